```python
import math, functools
import jax, jax.numpy as jnp
from jax import lax
import numpy as np

D_MODEL = 1024
BATCH = 8
SEQ = 2048
DEPTH = 2
DEC_BATCH = 32
DEC_SEQ = 1
PAST_LEN = 8192
PAGE_SIZE = 128

D_MIX = 2 * D_MODEL
ATT_WIDTH = D_MIX // 2
N_HEADS = 8
HEAD_DIM = ATT_WIDTH // N_HEADS
LRU_WIDTH = D_MIX // 4
LRU_BLOCKS = 8
LRU_BLOCK = LRU_WIDTH // LRU_BLOCKS
LRU_CONV = 4
LRU_C = 8.0
SC_WIDTH = D_MIX // 4
SC_CONV = 3
Q_BLOCK = 128
EPS = 1e-6
FORGET_BIAS = 2.0
SPLIT_SIZES = (ATT_WIDTH, ATT_WIDTH, ATT_WIDTH, N_HEADS, ATT_WIDTH,
               LRU_WIDTH, LRU_WIDTH,
               SC_WIDTH, SC_WIDTH, SC_WIDTH, SC_WIDTH)
D_IN = sum(SPLIT_SIZES)

kernel_name = "hymba_fox_rglru_shortconv_step"


def _rmsnorm(x, g):
    xf = x.astype(jnp.float32)
    out = xf * lax.rsqrt(jnp.mean(xf * xf, axis=-1, keepdims=True) + EPS)
    return (out * g.astype(jnp.float32)).astype(x.dtype)


def _causal_conv(u, buf, w):
    width = w.shape[0]
    t = u.shape[1]
    up = jnp.concatenate([buf.astype(u.dtype), u], axis=1)
    y = up[:, 0:t] * w[0]
    for j in range(1, width):
        y = y + up[:, j:j + t] * w[j]
    return y, up[:, t:]


def _rglru(xc, h0, w_a, b_a, w_x, b_x, lam):
    b, t, _ = xc.shape
    xf = xc.astype(jnp.float32)
    xb = xf.reshape(b, t, LRU_BLOCKS, LRU_BLOCK)
    r = jax.nn.sigmoid(jnp.einsum('btgi,gij->btgj', xb, w_a.astype(jnp.float32)).reshape(b, t, LRU_WIDTH) + b_a)
    i = jax.nn.sigmoid(jnp.einsum('btgi,gij->btgj', xb, w_x.astype(jnp.float32)).reshape(b, t, LRU_WIDTH) + b_x)
    log_a = -LRU_C * r * jax.nn.softplus(-lam.astype(jnp.float32))
    a = jnp.exp(log_a)
    drive = jnp.sqrt(-jnp.expm1(2.0 * log_a)) * (i * xf)
    drive = drive.at[:, 0].add(a[:, 0] * h0.astype(jnp.float32))

    def combine(left, right):
        a1, b1 = left
        a2, b2 = right
        return a1 * a2, a2 * b1 + b2

    _, h = lax.associative_scan(combine, (a, drive), axis=1)
    return h, h[:, -1]


def _fox_attend(q, c_q, q_pos, segs):
    scale = HEAD_DIM ** -0.5
    cq = jnp.transpose(c_q, (0, 2, 1))[..., :, None]
    scores = []
    for k, v, c_k, k_pos in segs:
        s = jnp.einsum('bqhd,bkhd->bhqk', q, k).astype(jnp.float32) * scale
        s = s + (cq - jnp.transpose(c_k, (0, 2, 1))[..., None, :])
        s = jnp.where(k_pos[None, :] <= q_pos[:, None], s, -jnp.inf)
        scores.append(s)
    p = jax.nn.softmax(jnp.concatenate(scores, axis=-1), axis=-1)
    out = None
    start = 0
    for k, v, c_k, k_pos in segs:
        n = k.shape[1]
        o = jnp.einsum('bhqk,bkhd->bqhd', p[..., start:start + n].astype(v.dtype), v)
        out = o if out is None else out + o
        start += n
    return out


def _fox_prompt(q, k, v, logf):
    b, s, h, d = q.shape
    c = jnp.cumsum(logf, axis=1)
    pos = jnp.arange(s)
    nb = s // Q_BLOCK
    qb = jnp.swapaxes(q.reshape(b, nb, Q_BLOCK, h, d), 0, 1)
    cb = jnp.swapaxes(c.reshape(b, nb, Q_BLOCK, h), 0, 1)
    pb = pos.reshape(nb, Q_BLOCK)
    out = lax.map(lambda blk: _fox_attend(blk[0], blk[1], blk[2], [(k, v, c, pos)]), (qb, cb, pb))
    return jnp.swapaxes(out, 0, 1).reshape(b, s, h, d)


def _fox_decode(q, k, v, logf, cache_k, cache_v, cache_logf, page_table, layer):
    bd, n_pages = page_table.shape
    past = n_pages * cache_k.shape[2]
    k_past = cache_k[layer, page_table].reshape(bd, past, N_HEADS, HEAD_DIM).astype(q.dtype)
    v_past = cache_v[layer, page_table].reshape(bd, past, N_HEADS, HEAD_DIM).astype(q.dtype)
    f_past = cache_logf[layer, page_table].reshape(bd, past, N_HEADS).astype(jnp.float32)
    c_past = jnp.cumsum(f_past, axis=1)
    c_new = c_past[:, -1:] + jnp.cumsum(logf, axis=1)
    pos_past = jnp.arange(past)
    pos_new = past + jnp.arange(q.shape[1])
    return _fox_attend(q, c_new, pos_new,
                       [(k_past, v_past, c_past, pos_past), (k, v, c_new, pos_new)])


def _layer(x, params, attn_fn, rg_h0, rg_buf, sc_buf):
    (g, w_in, b_f, rg_conv_w, rg_conv_b, rg_wa, rg_ba, rg_wx, rg_bx,
     rg_lam, sc_conv_w, w_out) = params
    b, t, _ = x.shape
    h = _rmsnorm(x, g)
    proj = h @ w_in
    idx = np.cumsum(SPLIT_SIZES)[:-1].tolist()
    q, k, v, fl, ga, xr, gr, bc, cc, hc, gc = jnp.split(proj, idx, axis=-1)
    q = q.reshape(b, t, N_HEADS, HEAD_DIM)
    k = k.reshape(b, t, N_HEADS, HEAD_DIM)
    v = v.reshape(b, t, N_HEADS, HEAD_DIM)
    logf = jax.nn.log_sigmoid(fl.astype(jnp.float32) + b_f.astype(jnp.float32))
    att = attn_fn(q, k, v, logf).reshape(b, t, ATT_WIDTH).astype(x.dtype)
    xc, rg_buf_new = _causal_conv(xr, rg_buf, rg_conv_w)
    xc = xc + rg_conv_b
    lru, h_last = _rglru(xc, rg_h0, rg_wa, rg_ba, rg_wx, rg_bx, rg_lam)
    lru = lru.astype(x.dtype)
    cu, sc_buf_new = _causal_conv(cc * hc, sc_buf, sc_conv_w)
    sc = bc * cu
    mixed = jnp.concatenate([att * jax.nn.silu(ga), lru * jax.nn.silu(gr), sc * jax.nn.silu(gc)], axis=-1)
    y = x + mixed @ w_out
    return y, (k, v, logf, h_last, rg_buf_new, sc_buf_new)


def setup_inputs(seed: int = 0) -> dict:
    key = jax.random.key(seed)
    ks = jax.random.split(key, 24)
    f32 = jnp.float32
    n_pages = PAST_LEN // PAGE_SIZE
    n_used = DEC_BATCH * n_pages
    n_pool = (n_used * 5) // 4
    x_prompt = jax.random.normal(ks[0], (BATCH, SEQ, D_MODEL), f32)
    x_sample = jax.random.normal(ks[1], (DEC_BATCH, DEC_SEQ, D_MODEL), f32)
    cache_k = jax.random.normal(ks[2], (DEPTH, n_pool, PAGE_SIZE, N_HEADS, HEAD_DIM), f32)
    cache_v = jax.random.normal(ks[3], (DEPTH, n_pool, PAGE_SIZE, N_HEADS, HEAD_DIM), f32)
    cache_logf = jax.nn.log_sigmoid(FORGET_BIAS + jax.random.normal(ks[4], (DEPTH, n_pool, PAGE_SIZE, N_HEADS), f32))
    state_rg_h = 0.5 * jax.random.normal(ks[5], (DEPTH, DEC_BATCH, LRU_WIDTH), f32)
    state_rg_conv = jax.random.normal(ks[6], (DEPTH, DEC_BATCH, LRU_CONV - 1, LRU_WIDTH), f32)
    state_sc_conv = jax.random.normal(ks[7], (DEPTH, DEC_BATCH, SC_CONV - 1, SC_WIDTH), f32)
    page_table = jax.random.permutation(ks[8], n_pool)[:n_used].reshape(DEC_BATCH, n_pages).astype(jnp.int32)
    norm_g = 1.0 + 0.1 * jax.random.normal(ks[9], (DEPTH, D_MODEL), f32)
    w_in = jax.random.normal(ks[10], (DEPTH, D_MODEL, D_IN), f32) * D_MODEL ** -0.5
    b_f = FORGET_BIAS + 0.5 * jax.random.normal(ks[11], (DEPTH, N_HEADS), f32)
    rg_conv_w = jax.random.normal(ks[12], (DEPTH, LRU_CONV, LRU_WIDTH), f32) * LRU_CONV ** -0.5
    rg_conv_b = 0.01 * jax.random.normal(ks[13], (DEPTH, LRU_WIDTH), f32)
    rg_wa = jax.random.normal(ks[14], (DEPTH, LRU_BLOCKS, LRU_BLOCK, LRU_BLOCK), f32) * LRU_BLOCK ** -0.5
    rg_ba = 0.01 * jax.random.normal(ks[15], (DEPTH, LRU_WIDTH), f32)
    rg_wx = jax.random.normal(ks[16], (DEPTH, LRU_BLOCKS, LRU_BLOCK, LRU_BLOCK), f32) * LRU_BLOCK ** -0.5
    rg_bx = 0.01 * jax.random.normal(ks[17], (DEPTH, LRU_WIDTH), f32)
    u = jax.random.uniform(ks[18], (DEPTH, LRU_WIDTH), f32, 0.9, 0.999)
    a0 = u ** (1.0 / LRU_C)
    rg_lambda = jnp.log(a0) - jnp.log1p(-a0)
    sc_conv_w = jax.random.normal(ks[19], (DEPTH, SC_CONV, SC_WIDTH), f32) * SC_CONV ** -0.5
    w_out = jax.random.normal(ks[20], (DEPTH, D_MIX, D_MODEL), f32) * D_MIX ** -0.5
    final_g = 1.0 + 0.1 * jax.random.normal(ks[21], (D_MODEL,), f32)
    return {"x_prompt": x_prompt, "x_sample": x_sample,
            "cache_k": cache_k, "cache_v": cache_v, "cache_logf": cache_logf,
            "state_rg_h": state_rg_h, "state_rg_conv": state_rg_conv, "state_sc_conv": state_sc_conv,
            "page_table": page_table,
            "norm_g": norm_g, "w_in": w_in, "b_f": b_f,
            "rg_conv_w": rg_conv_w, "rg_conv_b": rg_conv_b,
            "rg_wa": rg_wa, "rg_ba": rg_ba, "rg_wx": rg_wx, "rg_bx": rg_bx,
            "rg_lambda": rg_lambda, "sc_conv_w": sc_conv_w, "w_out": w_out,
            "final_g": final_g}


def reference(x_prompt, x_sample, cache_k, cache_v, cache_logf, state_rg_h, state_rg_conv,
              state_sc_conv, page_table, norm_g, w_in, b_f, rg_conv_w, rg_conv_b, rg_wa, rg_ba,
              rg_wx, rg_bx, rg_lambda, sc_conv_w, w_out, final_g):
    bp = x_prompt.shape[0]
    yp, ys = x_prompt, x_sample
    sp, ss = [], []
    for l in range(DEPTH):
        params = (norm_g[l], w_in[l], b_f[l], rg_conv_w[l], rg_conv_b[l], rg_wa[l], rg_ba[l],
                  rg_wx[l], rg_bx[l], rg_lambda[l], sc_conv_w[l], w_out[l])
        yp, st_p = _layer(yp, params, _fox_prompt,
                          jnp.zeros((bp, LRU_WIDTH), jnp.float32),
                          jnp.zeros((bp, LRU_CONV - 1, LRU_WIDTH), yp.dtype),
                          jnp.zeros((bp, SC_CONV - 1, SC_WIDTH), yp.dtype))
        dec_attn = functools.partial(_fox_decode, cache_k=cache_k, cache_v=cache_v,
                                     cache_logf=cache_logf, page_table=page_table, layer=l)
        ys, st_s = _layer(ys, params, dec_attn, state_rg_h[l], state_rg_conv[l], state_sc_conv[l])
        sp.append(st_p)
        ss.append(st_s)
    y_prompt = _rmsnorm(yp, final_g)
    y_sample = _rmsnorm(ys, final_g)
    new_k_prompt = jnp.stack([s[0] for s in sp])
    new_v_prompt = jnp.stack([s[1] for s in sp])
    new_logf_prompt = jnp.stack([s[2] for s in sp])
    new_rg_h_prompt = jnp.stack([s[3] for s in sp])
    new_rg_conv_prompt = jnp.stack([s[4] for s in sp])
    new_sc_conv_prompt = jnp.stack([s[5] for s in sp])
    new_k_sample = jnp.stack([s[0] for s in ss])
    new_v_sample = jnp.stack([s[1] for s in ss])
    new_logf_sample = jnp.stack([s[2] for s in ss])
    new_rg_h_sample = jnp.stack([s[3] for s in ss])
    new_rg_conv_sample = jnp.stack([s[4] for s in ss])
    new_sc_conv_sample = jnp.stack([s[5] for s in ss])
    return (y_prompt, y_sample,
            new_k_prompt, new_v_prompt, new_logf_prompt, new_rg_h_prompt, new_rg_conv_prompt, new_sc_conv_prompt,
            new_k_sample, new_v_sample, new_logf_sample, new_rg_h_sample, new_rg_conv_sample, new_sc_conv_sample)
```

```python
import functools

import jax
import jax.numpy as jnp
from jax import lax
from jax.experimental import pallas as pl
from jax.experimental.pallas import tpu as pltpu

F32 = jnp.float32
BF16 = jnp.bfloat16

N_HEADS = 8
HEAD_DIM = 128
ATT_WIDTH = N_HEADS * HEAD_DIM
LRU_WIDTH = 512
LRU_BLOCKS = 8
LRU_CONV = 4
LRU_C = 8.0
SC_WIDTH = 512
SC_CONV = 3
EPS = 1e-6
_OFF_FL = 3 * ATT_WIDTH
_OFF_GA = _OFF_FL + N_HEADS
_OFF_XR = _OFF_GA + ATT_WIDTH

LANES = 128
SUBLANES = 8
VMEM_LIMIT_BYTES = 56 * 1024 * 1024

PROJ_ROWS = 512
ATTN_BLOCK = 256
OUT_ROWS = 512
DEC_PAGES = 8


def _sigmoid(x):
    return 1.0 / (1.0 + jnp.exp(-x))


def _silu(x):
    return x * _sigmoid(x)


def _log_sigmoid(x):
    return jnp.minimum(x, 0.0) - jnp.log1p(jnp.exp(-jnp.abs(x)))


def _softplus(x):
    return jnp.maximum(x, 0.0) + jnp.log1p(jnp.exp(-jnp.abs(x)))


def _rms_scale(x):
    return lax.rsqrt(jnp.mean(x * x, axis=-1, keepdims=True) + EPS)


def _dot(a, b):
    return jnp.dot(a, b, preferred_element_type=F32)


def _lru_gates(xc, wa_ref, ba_ref, wx_ref, bx_ref, lam_ref):
    xb = xc.astype(BF16)
    r = _sigmoid(_dot(xb, wa_ref[...]) + ba_ref[...])
    i = _sigmoid(_dot(xb, wx_ref[...]) + bx_ref[...])
    log_a = (-LRU_C) * r * _softplus(-lam_ref[...])
    a = jnp.exp(log_a)
    drive = jnp.sqrt(1.0 - jnp.exp(2.0 * log_a)) * (i * xc)
    return a, drive


def _prompt_in_kernel(x_ref, g_ref, wqkvg_ref, wrest_ref, wfl_ref, bf_ref, cw_ref, cb_ref,
                      wa_ref, ba_ref, wx_ref, bx_ref, lam_ref, scw_ref,
                      q_ref, k_ref, v_ref, kb_ref, vb_ref, gsa_ref, mls_ref, logf_ref,
                      c_ref, ct_ref, rgh_ref, rgc_ref, scc_ref,
                      ext_rg, ext_sc, sa_ref, sb_ref, hcar_ref, ccar_ref, *, rows):
    t = pl.program_id(1)
    last_t = pl.num_programs(1) - 1

    @pl.when(t == 0)
    def _():
        ext_rg[0:SUBLANES, :] = jnp.zeros((SUBLANES, LRU_WIDTH), F32)
        ext_sc[0:SUBLANES, :] = jnp.zeros((SUBLANES, SC_WIDTH), F32)
        hcar_ref[...] = jnp.zeros_like(hcar_ref)
        ccar_ref[...] = jnp.zeros_like(ccar_ref)

    x = x_ref[...]
    h = (x * _rms_scale(x) * g_ref[...]).astype(BF16)

    q = _dot(h, wqkvg_ref[:, 0:ATT_WIDTH]) * (HEAD_DIM ** -0.5)
    q_ref[...] = q.astype(BF16)
    k = _dot(h, wqkvg_ref[:, ATT_WIDTH:2 * ATT_WIDTH])
    k_ref[...] = k
    kb_ref[...] = k.astype(BF16)
    v = _dot(h, wqkvg_ref[:, 2 * ATT_WIDTH:3 * ATT_WIDTH])
    v_ref[...] = v
    vb_ref[...] = v.astype(BF16)
    ga = _dot(h, wqkvg_ref[:, 3 * ATT_WIDTH:4 * ATT_WIDTH])
    gsa_ref[...] = _silu(ga).astype(BF16)

    logf = _log_sigmoid(_dot(h, wfl_ref[...]) + bf_ref[...])
    logf_ref[...] = logf[:, 0:N_HEADS]
    row = lax.broadcasted_iota(jnp.int32, (rows, LANES), 0)
    cs = logf
    shift = 1
    while shift < rows:
        cs = cs + jnp.where(row >= shift, pltpu.roll(cs, shift, 0), 0.0)
        shift *= 2
    cs = cs + ccar_ref[...]
    ccar_ref[...] = cs[rows - 1:rows, :]
    c_ref[...] = cs[:, 0:N_HEADS]
    cst = cs.T
    for j in range(rows // ATTN_BLOCK):
        ct_ref[j] = cst[0:N_HEADS, j * ATTN_BLOCK:(j + 1) * ATTN_BLOCK]

    xr = _dot(h, wrest_ref[:, 0:LRU_WIDTH])
    gr = _dot(h, wrest_ref[:, LRU_WIDTH:2 * LRU_WIDTH])
    ext_rg[SUBLANES:SUBLANES + rows, :] = xr
    xc = cb_ref[...] + cw_ref[LRU_CONV - 1:LRU_CONV, :] * xr
    for j in range(LRU_CONV - 1):
        o = SUBLANES - (LRU_CONV - 1) + j
        xc = xc + cw_ref[j:j + 1, :] * ext_rg[o:o + rows, :]
    tail = ext_rg[rows:rows + SUBLANES, :]
    ext_rg[0:SUBLANES, :] = tail

    a, drive = _lru_gates(xc, wa_ref, ba_ref, wx_ref, bx_ref, lam_ref)
    rmod = lax.broadcasted_iota(jnp.int32, (rows, LRU_WIDTH), 0) % SUBLANES
    shift = 1
    while shift < SUBLANES:
        keep = rmod >= shift
        a_sh = pltpu.roll(a, shift, 0)
        d_sh = pltpu.roll(drive, shift, 0)
        drive = jnp.where(keep, a * d_sh + drive, drive)
        a = jnp.where(keep, a * a_sh, a)
        shift *= 2
    sa_ref[...] = a
    sb_ref[...] = drive

    def group_step(gi, hprev):
        r0 = pl.multiple_of(gi * SUBLANES, SUBLANES)
        hg = sa_ref[pl.ds(r0, SUBLANES), :] * hprev + sb_ref[pl.ds(r0, SUBLANES), :]
        sb_ref[pl.ds(r0, SUBLANES), :] = hg
        return hg[SUBLANES - 1:SUBLANES, :]

    hlast = lax.fori_loop(0, rows // SUBLANES, group_step, hcar_ref[...])
    hcar_ref[...] = hlast
    lru = sb_ref[...]
    mls_ref[:, 0:LRU_WIDTH] = (lru * _silu(gr)).astype(BF16)

    o0 = 2 * LRU_WIDTH
    bc = _dot(h, wrest_ref[:, o0:o0 + SC_WIDTH])
    cc = _dot(h, wrest_ref[:, o0 + SC_WIDTH:o0 + 2 * SC_WIDTH])
    hc = _dot(h, wrest_ref[:, o0 + 2 * SC_WIDTH:o0 + 3 * SC_WIDTH])
    gc = _dot(h, wrest_ref[:, o0 + 3 * SC_WIDTH:o0 + 4 * SC_WIDTH])
    u = cc * hc
    ext_sc[SUBLANES:SUBLANES + rows, :] = u
    cu = scw_ref[SC_CONV - 1:SC_CONV, :] * u
    for j in range(SC_CONV - 1):
        o = SUBLANES - (SC_CONV - 1) + j
        cu = cu + scw_ref[j:j + 1, :] * ext_sc[o:o + rows, :]
    tail_sc = ext_sc[rows:rows + SUBLANES, :]
    ext_sc[0:SUBLANES, :] = tail_sc
    mls_ref[:, LRU_WIDTH:LRU_WIDTH + SC_WIDTH] = (bc * cu * _silu(gc)).astype(BF16)

    @pl.when(t == last_t)
    def _():
        rgh_ref[...] = hlast
        rgc_ref[...] = tail[SUBLANES - (LRU_CONV - 1):SUBLANES, :]
        scc_ref[...] = tail_sc[SUBLANES - (SC_CONV - 1):SUBLANES, :]


def _const_spec(shape):
    nd = len(shape)
    return pl.BlockSpec(shape, lambda *_: (0,) * nd, pipeline_mode=pl.Buffered(1))


def _prompt_in(x, lw):
    b, s, d = x.shape
    rows = PROJ_ROWS
    nt = s // rows
    nblk = rows // ATTN_BLOCK
    row_spec = lambda w: pl.BlockSpec((None, rows, w), lambda bi, ti: (bi, ti, 0))
    last_spec = lambda r, w: pl.BlockSpec((None, r, w), lambda bi, ti: (bi, 0, 0))
    weights = (lw["g"], lw["w_qkvg"], lw["w_rest"], lw["w_fl"], lw["b_f"], lw["rg_conv_w"],
               lw["rg_conv_b"], lw["wa"], lw["ba"], lw["wx"], lw["bx"], lw["lam"], lw["sc_conv_w"])
    out_shape = (
        jax.ShapeDtypeStruct((b, s, ATT_WIDTH), BF16),
        jax.ShapeDtypeStruct((b, s, ATT_WIDTH), F32),
        jax.ShapeDtypeStruct((b, s, ATT_WIDTH), F32),
        jax.ShapeDtypeStruct((b, s, ATT_WIDTH), BF16),
        jax.ShapeDtypeStruct((b, s, ATT_WIDTH), BF16),
        jax.ShapeDtypeStruct((b, s, ATT_WIDTH), BF16),
        jax.ShapeDtypeStruct((b, s, LRU_WIDTH + SC_WIDTH), BF16),
        jax.ShapeDtypeStruct((b, s, N_HEADS), F32),
        jax.ShapeDtypeStruct((b, s, N_HEADS), F32),
        jax.ShapeDtypeStruct((b, s // ATTN_BLOCK, N_HEADS, ATTN_BLOCK), F32),
        jax.ShapeDtypeStruct((b, 1, LRU_WIDTH), F32),
        jax.ShapeDtypeStruct((b, LRU_CONV - 1, LRU_WIDTH), F32),
        jax.ShapeDtypeStruct((b, SC_CONV - 1, SC_WIDTH), F32),
    )
    out_specs = (
        row_spec(ATT_WIDTH), row_spec(ATT_WIDTH), row_spec(ATT_WIDTH), row_spec(ATT_WIDTH),
        row_spec(ATT_WIDTH), row_spec(ATT_WIDTH), row_spec(LRU_WIDTH + SC_WIDTH),
        row_spec(N_HEADS), row_spec(N_HEADS),
        pl.BlockSpec((None, nblk, N_HEADS, ATTN_BLOCK), lambda bi, ti: (bi, ti, 0, 0)),
        last_spec(1, LRU_WIDTH), last_spec(LRU_CONV - 1, LRU_WIDTH), last_spec(SC_CONV - 1, SC_WIDTH),
    )
    return pl.pallas_call(
        functools.partial(_prompt_in_kernel, rows=rows),
        out_shape=out_shape,
        grid=(b, nt),
        in_specs=[row_spec(d)] + [_const_spec(w.shape) for w in weights],
        out_specs=out_specs,
        scratch_shapes=[
            pltpu.VMEM((rows + SUBLANES, LRU_WIDTH), F32),
            pltpu.VMEM((rows + SUBLANES, SC_WIDTH), F32),
            pltpu.VMEM((rows, LRU_WIDTH), F32),
            pltpu.VMEM((rows, LRU_WIDTH), F32),
            pltpu.VMEM((1, LRU_WIDTH), F32),
            pltpu.VMEM((1, LANES), F32),
        ],
        compiler_params=pltpu.CompilerParams(
            dimension_semantics=("arbitrary", "arbitrary"), vmem_limit_bytes=VMEM_LIMIT_BYTES),
        name="prompt_in",
    )(x, *weights)


def _prompt_attn_kernel(q_ref, kb_ref, vb_ref, c_ref, ct_ref, gsa_ref, o_ref):
    qi = pl.program_id(1)
    blk = ATTN_BLOCK
    row = lax.broadcasted_iota(jnp.int32, (blk, blk), 0)
    col = lax.broadcasted_iota(jnp.int32, (blk, blk), 1)
    causal = row >= col
    for hd in range(N_HEADS):
        lo, hi = hd * HEAD_DIM, (hd + 1) * HEAD_DIM
        qh = q_ref[:, lo:hi]
        cq = c_ref[:, hd:hd + 1]

        def kv_step(j, carry, masked, lo=lo, hi=hi, hd=hd, qh=qh, cq=cq):
            m, l, acc = carry
            r0 = pl.multiple_of(j * blk, blk)
            kblk = kb_ref[pl.ds(r0, blk), lo:hi]
            s = lax.dot_general(qh, kblk, (((1,), (1,)), ((), ())), preferred_element_type=F32)
            s = s + (cq - ct_ref[j, hd:hd + 1, :])
            if masked:
                s = jnp.where(causal, s, -jnp.inf)
            m_new = jnp.maximum(m, jnp.max(s, axis=1, keepdims=True))
            alpha = jnp.exp(m - m_new)
            p = jnp.exp(s - m_new)
            l = alpha * l + jnp.sum(p, axis=1, keepdims=True)
            vblk = vb_ref[pl.ds(r0, blk), lo:hi]
            acc = alpha * acc + _dot(p.astype(BF16), vblk)
            return m_new, l, acc

        init = (jnp.full((blk, 1), -jnp.inf, F32), jnp.zeros((blk, 1), F32),
                jnp.zeros((blk, HEAD_DIM), F32))
        carry = lax.fori_loop(0, qi, functools.partial(kv_step, masked=False), init)
        _, l, acc = kv_step(qi, carry, True)
        o_ref[:, lo:hi] = (acc / l * gsa_ref[:, lo:hi].astype(F32)).astype(BF16)


def _prompt_attn(q, kb, vb, c, ct, gsa):
    b, s, _ = q.shape
    blk = ATTN_BLOCK
    nq = s // blk
    qspec = pl.BlockSpec((None, blk, ATT_WIDTH), lambda bi, qi: (bi, qi, 0))
    kvspec = pl.BlockSpec((None, s, ATT_WIDTH), lambda bi, qi: (bi, 0, 0))
    return pl.pallas_call(
        _prompt_attn_kernel,
        out_shape=jax.ShapeDtypeStruct((b, s, ATT_WIDTH), BF16),
        grid=(b, nq),
        in_specs=[qspec, kvspec, kvspec,
                  pl.BlockSpec((None, blk, N_HEADS), lambda bi, qi: (bi, qi, 0)),
                  pl.BlockSpec((None, nq, N_HEADS, blk), lambda bi, qi: (bi, 0, 0, 0)),
                  qspec],
        out_specs=qspec,
        compiler_params=pltpu.CompilerParams(
            dimension_semantics=("arbitrary", "arbitrary"), vmem_limit_bytes=VMEM_LIMIT_BYTES),
        name="prompt_attn",
    )(q, kb, vb, c, ct, gsa)


def _out_kernel(x_ref, ma_ref, mls_ref, wo_ref, fg_ref, y_ref, *, final_norm):
    y = x_ref[...] + _dot(ma_ref[...].astype(BF16), wo_ref[0:ATT_WIDTH, :])
    y = y + _dot(mls_ref[...], wo_ref[ATT_WIDTH:, :])
    if final_norm:
        y = y * _rms_scale(y) * fg_ref[...]
    y_ref[...] = y


def _out_proj(x2, ma2, mls2, wo, fg, *, rows, final_norm):
    n, d = x2.shape
    spec = lambda w: pl.BlockSpec((rows, w), lambda i: (i, 0))
    return pl.pallas_call(
        functools.partial(_out_kernel, final_norm=final_norm),
        out_shape=jax.ShapeDtypeStruct((n, d), F32),
        grid=(n // rows,),
        in_specs=[spec(d), spec(ATT_WIDTH), spec(LRU_WIDTH + SC_WIDTH),
                  _const_spec(wo.shape), _const_spec(fg.shape)],
        out_specs=spec(d),
        compiler_params=pltpu.CompilerParams(
            dimension_semantics=("arbitrary",), vmem_limit_bytes=VMEM_LIMIT_BYTES),
        name="out_proj",
    )(x2, ma2, mls2, wo, fg)


def _sample_in_kernel(x_ref, g_ref, wqkvg_ref, wrest_ref, wfl_ref, bf_ref, cw_ref, cb_ref,
                      wa_ref, ba_ref, wx_ref, bx_ref, lam_ref, scw_ref, h0_ref, rgbuf_ref, scbuf_ref,
                      q_ref, k_ref, v_ref, gsa_ref, mls_ref, logf_ref, rgh_ref, rgc_ref, scc_ref):
    x = x_ref[...]
    h = (x * _rms_scale(x) * g_ref[...]).astype(BF16)
    q_ref[...] = _dot(h, wqkvg_ref[:, 0:ATT_WIDTH]) * (HEAD_DIM ** -0.5)
    k_ref[...] = _dot(h, wqkvg_ref[:, ATT_WIDTH:2 * ATT_WIDTH])
    v_ref[...] = _dot(h, wqkvg_ref[:, 2 * ATT_WIDTH:3 * ATT_WIDTH])
    gsa_ref[...] = _silu(_dot(h, wqkvg_ref[:, 3 * ATT_WIDTH:4 * ATT_WIDTH]))
    logf = _log_sigmoid(_dot(h, wfl_ref[...]) + bf_ref[...])
    logf_ref[...] = logf[:, 0:N_HEADS]

    xr = _dot(h, wrest_ref[:, 0:LRU_WIDTH])
    gr = _dot(h, wrest_ref[:, LRU_WIDTH:2 * LRU_WIDTH])
    xc = cb_ref[...] + cw_ref[LRU_CONV - 1:LRU_CONV, :] * xr
    for j in range(LRU_CONV - 1):
        xc = xc + cw_ref[j:j + 1, :] * rgbuf_ref[j]
    for j in range(LRU_CONV - 2):
        rgc_ref[j] = rgbuf_ref[j + 1]
    rgc_ref[LRU_CONV - 2] = xr
    a, drive = _lru_gates(xc, wa_ref, ba_ref, wx_ref, bx_ref, lam_ref)
    hn = a * h0_ref[...] + drive
    rgh_ref[...] = hn
    mls_ref[:, 0:LRU_WIDTH] = (hn * _silu(gr)).astype(BF16)

    o0 = 2 * LRU_WIDTH
    bc = _dot(h, wrest_ref[:, o0:o0 + SC_WIDTH])
    cc = _dot(h, wrest_ref[:, o0 + SC_WIDTH:o0 + 2 * SC_WIDTH])
    hc = _dot(h, wrest_ref[:, o0 + 2 * SC_WIDTH:o0 + 3 * SC_WIDTH])
    gc = _dot(h, wrest_ref[:, o0 + 3 * SC_WIDTH:o0 + 4 * SC_WIDTH])
    u = cc * hc
    cu = scw_ref[SC_CONV - 1:SC_CONV, :] * u
    for j in range(SC_CONV - 1):
        cu = cu + scw_ref[j:j + 1, :] * scbuf_ref[j]
    for j in range(SC_CONV - 2):
        scc_ref[j] = scbuf_ref[j + 1]
    scc_ref[SC_CONV - 2] = u
    mls_ref[:, LRU_WIDTH:LRU_WIDTH + SC_WIDTH] = (bc * cu * _silu(gc)).astype(BF16)


def _sample_in(x2, lw, h0, rgbuf_t, scbuf_t):
    n, _ = x2.shape
    weights = (lw["g"], lw["w_qkvg"], lw["w_rest"], lw["w_fl"], lw["b_f"], lw["rg_conv_w"],
               lw["rg_conv_b"], lw["wa"], lw["ba"], lw["wx"], lw["bx"], lw["lam"], lw["sc_conv_w"])
    out_shape = (
        jax.ShapeDtypeStruct((n, ATT_WIDTH), F32),
        jax.ShapeDtypeStruct((n, ATT_WIDTH), F32),
        jax.ShapeDtypeStruct((n, ATT_WIDTH), F32),
        jax.ShapeDtypeStruct((n, ATT_WIDTH), F32),
        jax.ShapeDtypeStruct((n, LRU_WIDTH + SC_WIDTH), BF16),
        jax.ShapeDtypeStruct((n, N_HEADS), F32),
        jax.ShapeDtypeStruct((n, LRU_WIDTH), F32),
        jax.ShapeDtypeStruct((LRU_CONV - 1, n, LRU_WIDTH), F32),
        jax.ShapeDtypeStruct((SC_CONV - 1, n, SC_WIDTH), F32),
    )
    operands = (x2,) + weights + (h0, rgbuf_t, scbuf_t)
    full = lambda shape: pl.BlockSpec(shape, lambda i, nd=len(shape): (0,) * nd)
    return pl.pallas_call(
        _sample_in_kernel,
        out_shape=out_shape,
        grid=(1,),
        in_specs=[_const_spec(o.shape) for o in operands],
        out_specs=tuple(full(o.shape) for o in out_shape),
        compiler_params=pltpu.CompilerParams(
            dimension_semantics=("arbitrary",), vmem_limit_bytes=VMEM_LIMIT_BYTES),
        name="sample_in",
    )(*operands)


def _decode_kernel(pt_ref, q_ref, kn_ref, vn_ref, fn_ref, gsa_ref, *refs):
    npg = DEC_PAGES
    k_refs = refs[0:npg]
    v_refs = refs[npg:2 * npg]
    f_refs = refs[2 * npg:3 * npg]
    o_ref = refs[3 * npg]
    qrow_ref, m_ref, l_ref, acc_ref, tot_ref, fpad_ref = refs[3 * npg + 1:]
    j = pl.program_id(1)
    head_of_lane = lax.broadcasted_iota(jnp.int32, (N_HEADS, ATT_WIDTH), 1) // HEAD_DIM
    head_of_row = lax.broadcasted_iota(jnp.int32, (N_HEADS, ATT_WIDTH), 0)
    own = head_of_lane == head_of_row

    @pl.when(j == 0)
    def _():
        qr = jnp.where(own, jnp.broadcast_to(q_ref[...], (N_HEADS, ATT_WIDTH)), 0.0)
        qrow_ref[...] = qr.astype(BF16)
        m_ref[...] = jnp.sum(qr * kn_ref[...], axis=1, keepdims=True)
        l_ref[...] = jnp.ones_like(l_ref)
        acc_ref[...] = jnp.broadcast_to(vn_ref[...], (N_HEADS, ATT_WIDTH))
        tot_ref[...] = jnp.broadcast_to(fn_ref[...], (N_HEADS, LANES))
        fpad_ref[...] = jnp.zeros_like(fpad_ref)

    ri = lax.broadcasted_iota(jnp.int32, (LANES, 2 * LANES), 0)
    ci = lax.broadcasted_iota(jnp.int32, (LANES, 2 * LANES), 1)
    u2 = jnp.where((ri > ci) | (ci >= LANES), 1.0, 0.0).astype(F32)

    qrow = qrow_ref[...]
    tot = tot_ref[...]
    scores = []
    for i in range(npg):
        fpad_ref[:, 0:N_HEADS] = f_refs[i][...]
        ft = fpad_ref[...].T[0:N_HEADS, :]
        suf = jnp.dot(ft, u2, precision=lax.Precision.HIGHEST, preferred_element_type=F32)
        kpage = k_refs[i][...].astype(BF16)
        s = lax.dot_general(qrow, kpage, (((1,), (1,)), ((), ())), preferred_element_type=F32)
        scores.append(s + tot + suf[:, 0:LANES])
        tot = tot + suf[:, LANES:2 * LANES]
    tot_ref[...] = tot
    s_all = jnp.concatenate(scores, axis=1)
    m_old = m_ref[...]
    m_new = jnp.maximum(m_old, jnp.max(s_all, axis=1, keepdims=True))
    alpha = jnp.exp(m_old - m_new)
    p = jnp.exp(s_all - m_new)
    l_ref[...] = alpha * l_ref[...] + jnp.sum(p, axis=1, keepdims=True)
    m_ref[...] = m_new
    pb = p.astype(BF16)
    acc = alpha * acc_ref[...]
    for i in range(npg):
        acc = acc + _dot(pb[:, i * LANES:(i + 1) * LANES], v_refs[i][...].astype(BF16))
    acc_ref[...] = acc

    @pl.when(j == pl.num_programs(1) - 1)
    def _():
        att = jnp.sum(jnp.where(own, acc / l_ref[...], 0.0), axis=0, keepdims=True)
        o_ref[...] = att * gsa_ref[...]


def _decode_attn(layer, page_table, cache_k4, cache_v4, cache_logf, q3, kn3, vn3, fn3, gsa3):
    n, n_pages = page_table.shape
    page = cache_k4.shape[2]
    assert page == LANES and n_pages % DEC_PAGES == 0
    steps = n_pages // DEC_PAGES
    tok = pl.BlockSpec((None, 1, ATT_WIDTH), lambda bi, j, pt: (bi, 0, 0))

    def page_spec(i, width):
        def imap(bi, j, pt):
            return (layer, pt[bi, n_pages - 1 - (j * DEC_PAGES + i)], 0, 0)
        return pl.BlockSpec((None, None, page, width), imap)

    in_specs = ([tok, tok, tok, pl.BlockSpec((None, N_HEADS, 1), lambda bi, j, pt: (bi, 0, 0)), tok]
                + [page_spec(i, ATT_WIDTH) for i in range(DEC_PAGES)]
                + [page_spec(i, ATT_WIDTH) for i in range(DEC_PAGES)]
                + [page_spec(i, N_HEADS) for i in range(DEC_PAGES)])
    grid_spec = pltpu.PrefetchScalarGridSpec(
        num_scalar_prefetch=1,
        grid=(n, steps),
        in_specs=in_specs,
        out_specs=tok,
        scratch_shapes=[
            pltpu.VMEM((N_HEADS, ATT_WIDTH), BF16),
            pltpu.VMEM((N_HEADS, 1), F32),
            pltpu.VMEM((N_HEADS, 1), F32),
            pltpu.VMEM((N_HEADS, ATT_WIDTH), F32),
            pltpu.VMEM((N_HEADS, LANES), F32),
            pltpu.VMEM((LANES, LANES), F32),
        ],
    )
    return pl.pallas_call(
        _decode_kernel,
        out_shape=jax.ShapeDtypeStruct((n, 1, ATT_WIDTH), F32),
        grid_spec=grid_spec,
        compiler_params=pltpu.CompilerParams(
            dimension_semantics=("arbitrary", "arbitrary"), vmem_limit_bytes=VMEM_LIMIT_BYTES),
        name="decode_attn",
    )(page_table, q3, kn3, vn3, fn3, gsa3,
      *([cache_k4] * DEC_PAGES), *([cache_v4] * DEC_PAGES), *([cache_logf] * DEC_PAGES))


def _block_diag(w):
    g, i, j = w.shape
    eye = jnp.eye(g, dtype=w.dtype)
    return (eye[:, None, :, None] * w[:, :, None, :]).reshape(g * i, g * j)


def _layer_weights(l, norm_g, w_in, b_f, rg_conv_w, rg_conv_b, rg_wa, rg_ba, rg_wx, rg_bx,
                   rg_lambda, sc_conv_w):
    w = w_in[l]
    row = lambda a: a.reshape(1, -1).astype(F32)
    return dict(
        g=row(norm_g[l]),
        w_qkvg=jnp.concatenate([w[:, 0:_OFF_FL], w[:, _OFF_GA:_OFF_XR]], axis=1).astype(BF16),
        w_rest=w[:, _OFF_XR:].astype(BF16),
        w_fl=jnp.pad(w[:, _OFF_FL:_OFF_GA], ((0, 0), (0, LANES - N_HEADS))).astype(BF16),
        b_f=jnp.pad(row(b_f[l]), ((0, 0), (0, LANES - N_HEADS))),
        rg_conv_w=rg_conv_w[l].astype(F32), rg_conv_b=row(rg_conv_b[l]),
        wa=_block_diag(rg_wa[l]).astype(BF16), ba=row(rg_ba[l]),
        wx=_block_diag(rg_wx[l]).astype(BF16), bx=row(rg_bx[l]),
        lam=row(rg_lambda[l]), sc_conv_w=sc_conv_w[l].astype(F32),
    )


def kernel(x_prompt, x_sample, cache_k, cache_v, cache_logf, state_rg_h, state_rg_conv, state_sc_conv, page_table, norm_g, w_in, b_f, rg_conv_w, rg_conv_b, rg_wa, rg_ba, rg_wx, rg_bx, rg_lambda, sc_conv_w, w_out, final_g):
    depth = w_in.shape[0]
    b, s, d = x_prompt.shape
    n = x_sample.shape[0]
    n_pool, page = cache_k.shape[1], cache_k.shape[2]
    cache_k4 = cache_k.reshape(depth, n_pool, page, ATT_WIDTH)
    cache_v4 = cache_v.reshape(depth, n_pool, page, ATT_WIDTH)
    fg = final_g.reshape(1, d).astype(F32)

    yp = x_prompt
    ys = x_sample.reshape(n, d)
    sp, ss = [], []
    for l in range(depth):
        lw = _layer_weights(l, norm_g, w_in, b_f, rg_conv_w, rg_conv_b, rg_wa, rg_ba, rg_wx,
                            rg_bx, rg_lambda, sc_conv_w)
        wo = w_out[l].astype(BF16)
        last = l == depth - 1

        q, k, v, kb, vb, gsa, mls, logf, c, ct, rgh, rgc, scc = _prompt_in(yp, lw)
        ma = _prompt_attn(q, kb, vb, c, ct, gsa)
        yp = _out_proj(yp.reshape(b * s, d), ma.reshape(b * s, ATT_WIDTH),
                       mls.reshape(b * s, LRU_WIDTH + SC_WIDTH), wo, fg,
                       rows=OUT_ROWS, final_norm=last).reshape(b, s, d)
        sp.append((k.reshape(b, s, N_HEADS, HEAD_DIM), v.reshape(b, s, N_HEADS, HEAD_DIM), logf,
                   rgh.reshape(b, LRU_WIDTH), rgc, scc))

        qs, ks, vs, gsas, mlss, logfs, rghs, rgcs, sccs = _sample_in(
            ys, lw, state_rg_h[l], jnp.swapaxes(state_rg_conv[l], 0, 1),
            jnp.swapaxes(state_sc_conv[l], 0, 1))
        tok3 = lambda a: a.reshape(n, 1, ATT_WIDTH)
        mas = _decode_attn(l, page_table, cache_k4, cache_v4, cache_logf, tok3(qs), tok3(ks),
                           tok3(vs), logfs.reshape(n, N_HEADS, 1), tok3(gsas))
        ys = _out_proj(ys, mas.reshape(n, ATT_WIDTH), mlss, wo, fg, rows=n, final_norm=last)
        ss.append((ks.reshape(n, 1, N_HEADS, HEAD_DIM), vs.reshape(n, 1, N_HEADS, HEAD_DIM),
                   logfs.reshape(n, 1, N_HEADS), rghs, jnp.swapaxes(rgcs, 0, 1),
                   jnp.swapaxes(sccs, 0, 1)))

    stack = lambda states, i: jnp.stack([st[i] for st in states])
    return (yp, ys.reshape(n, 1, d),
            stack(sp, 0), stack(sp, 1), stack(sp, 2), stack(sp, 3), stack(sp, 4), stack(sp, 5),
            stack(ss, 0), stack(ss, 1), stack(ss, 2), stack(ss, 3), stack(ss, 4), stack(ss, 5))
```

```python
import functools

import numpy as np
import jax
import jax.numpy as jnp
from jax import lax
from jax.experimental import pallas as pl
from jax.experimental.pallas import tpu as pltpu

F32 = jnp.float32
BF16 = jnp.bfloat16

N_HEADS = 8
HEAD_DIM = 128
ATT_WIDTH = N_HEADS * HEAD_DIM
LRU_WIDTH = 512
LRU_BLOCKS = 8
LRU_CONV = 4
LRU_C = 8.0
SC_WIDTH = 512
SC_CONV = 3
EPS = 1e-6
_OFF_FL = 3 * ATT_WIDTH
_OFF_GA = _OFF_FL + N_HEADS
_OFF_XR = _OFF_GA + ATT_WIDTH

LANES = 128
SUBLANES = 8
VMEM_LIMIT_BYTES = 60 * 1024 * 1024

PROJ_ROWS = 512
ATTN_BLOCK = 512
OUT_ROWS = 512
DEC_PAGES = 8

LOG2E = 1.4426950408889634
AUG_DIM = 2 * HEAD_DIM
_SPLIT_LANES = (0, 16, 32)
_ONES_LANE = N_HEADS


def _sigmoid(x):
    return 1.0 / (1.0 + jnp.exp(-x))


def _silu(x):
    return x * _sigmoid(x)


def _log_sigmoid(x):
    return jnp.minimum(x, 0.0) - jnp.log1p(jnp.exp(-jnp.abs(x)))


def _softplus(x):
    return jnp.maximum(x, 0.0) + jnp.log1p(jnp.exp(-jnp.abs(x)))


def _rms_scale(x):
    return lax.rsqrt(jnp.mean(x * x, axis=-1, keepdims=True) + EPS)


def _dot(a, b):
    return jnp.dot(a, b, preferred_element_type=F32)


def _dot_nt(a, b):
    return lax.dot_general(a, b, (((1,), (1,)), ((), ())), preferred_element_type=F32)


def _split3(x):
    hi = x.astype(BF16).astype(F32)
    r = x - hi
    mid = r.astype(BF16).astype(F32)
    lo = (r - mid).astype(BF16).astype(F32)
    return hi, mid, lo


def _lru_gates(xc, wa_ref, ba_ref, wx_ref, bx_ref, lam_ref):
    xb = xc.astype(BF16)
    r = _sigmoid(_dot(xb, wa_ref[...]) + ba_ref[...])
    i = _sigmoid(_dot(xb, wx_ref[...]) + bx_ref[...])
    log_a = (-LRU_C) * r * _softplus(-lam_ref[...])
    a = jnp.exp(log_a)
    drive = jnp.sqrt(1.0 - jnp.exp(2.0 * log_a)) * (i * xc)
    return a, drive


def _bias_feature_matrix():
    m = np.zeros((LANES, 2 * ATT_WIDTH), np.float32)
    for h in range(N_HEADS):
        for part, lane0 in enumerate(_SPLIT_LANES):
            m[lane0 + h, h * HEAD_DIM + part] = 1.0
            m[lane0 + h, ATT_WIDTH + h * HEAD_DIM + 3 + part] = -1.0
            m[_ONES_LANE, h * HEAD_DIM + 3 + part] = 1.0
            m[_ONES_LANE, ATT_WIDTH + h * HEAD_DIM + part] = 1.0
    return jnp.asarray(m, BF16)


def _prompt_in_kernel(x_ref, g_ref, wqkvg_ref, wrest_ref, wfl_ref, bf_ref, sel_ref, cw_ref, cb_ref,
                      wa_ref, ba_ref, wx_ref, bx_ref, lam_ref, scw_ref,
                      qp_ref, kp_ref, k_ref, v_ref, vt_ref, gsa_ref, mls_ref, logf_ref,
                      rgh_ref, rgc_ref, scc_ref,
                      ext_rg, ext_sc, sa_ref, sb_ref, hcar_ref, ccar_ref, *, rows):
    t = pl.program_id(1)
    last_t = pl.num_programs(1) - 1

    @pl.when(t == 0)
    def _():
        ext_rg[0:SUBLANES, :] = jnp.zeros((SUBLANES, LRU_WIDTH), F32)
        ext_sc[0:SUBLANES, :] = jnp.zeros((SUBLANES, SC_WIDTH), F32)
        hcar_ref[...] = jnp.zeros_like(hcar_ref)
        ccar_ref[...] = jnp.zeros_like(ccar_ref)

    x = x_ref[...]
    h = (x * _rms_scale(x) * g_ref[...]).astype(BF16)

    logf = _log_sigmoid(_dot(h, wfl_ref[...]) + bf_ref[...])
    logf_ref[...] = logf[:, 0:N_HEADS]
    row = lax.broadcasted_iota(jnp.int32, (rows, LANES), 0)
    lane = lax.broadcasted_iota(jnp.int32, (rows, LANES), 1)
    cs = logf
    shift = 1
    while shift < rows:
        cs = cs + jnp.where(row >= shift, pltpu.roll(cs, shift, 0), 0.0)
        shift *= 2
    cs = cs + ccar_ref[...]
    ccar_ref[...] = cs[rows - 1:rows, :]
    cx = jnp.where(lane < N_HEADS, cs * LOG2E, jnp.where(lane == _ONES_LANE, 1.0, 0.0))
    hi, mid, lo = _split3(cx)
    packed = hi + pltpu.roll(mid, _SPLIT_LANES[1], 1) + pltpu.roll(lo, _SPLIT_LANES[2], 1)
    aug = _dot(packed.astype(BF16), sel_ref[...])

    q = _dot(h, wqkvg_ref[:, 0:ATT_WIDTH]) * (HEAD_DIM ** -0.5 * LOG2E)
    k = _dot(h, wqkvg_ref[:, ATT_WIDTH:2 * ATT_WIDTH])
    k_ref[...] = k
    for hd in range(N_HEADS):
        src = slice(hd * HEAD_DIM, (hd + 1) * HEAD_DIM)
        qp_ref[:, hd * AUG_DIM:hd * AUG_DIM + HEAD_DIM] = q[:, src].astype(BF16)
        qp_ref[:, hd * AUG_DIM + HEAD_DIM:(hd + 1) * AUG_DIM] = aug[:, src].astype(BF16)
        kp_ref[:, hd * AUG_DIM:hd * AUG_DIM + HEAD_DIM] = k[:, src].astype(BF16)
        kp_ref[:, hd * AUG_DIM + HEAD_DIM:(hd + 1) * AUG_DIM] = (
            aug[:, ATT_WIDTH + hd * HEAD_DIM:ATT_WIDTH + (hd + 1) * HEAD_DIM].astype(BF16))
    v = _dot(h, wqkvg_ref[:, 2 * ATT_WIDTH:3 * ATT_WIDTH])
    v_ref[...] = v
    vt_ref[...] = v.T.astype(BF16)
    ga = _dot(h, wqkvg_ref[:, 3 * ATT_WIDTH:4 * ATT_WIDTH])
    gsa_ref[...] = _silu(ga).astype(BF16)

    xr = _dot(h, wrest_ref[:, 0:LRU_WIDTH])
    gr = _dot(h, wrest_ref[:, LRU_WIDTH:2 * LRU_WIDTH])
    ext_rg[SUBLANES:SUBLANES + rows, :] = xr
    xc = cb_ref[...] + cw_ref[LRU_CONV - 1:LRU_CONV, :] * xr
    for j in range(LRU_CONV - 1):
        o = SUBLANES - (LRU_CONV - 1) + j
        xc = xc + cw_ref[j:j + 1, :] * ext_rg[o:o + rows, :]
    tail = ext_rg[rows:rows + SUBLANES, :]
    ext_rg[0:SUBLANES, :] = tail

    a, drive = _lru_gates(xc, wa_ref, ba_ref, wx_ref, bx_ref, lam_ref)
    rmod = lax.broadcasted_iota(jnp.int32, (rows, LRU_WIDTH), 0) % SUBLANES
    shift = 1
    while shift < SUBLANES:
        keep = rmod >= shift
        a_sh = pltpu.roll(a, shift, 0)
        d_sh = pltpu.roll(drive, shift, 0)
        drive = jnp.where(keep, a * d_sh + drive, drive)
        a = jnp.where(keep, a * a_sh, a)
        shift *= 2
    sa_ref[...] = a
    sb_ref[...] = drive

    def group_step(gi, hprev):
        r0 = pl.multiple_of(gi * SUBLANES, SUBLANES)
        hg = sa_ref[pl.ds(r0, SUBLANES), :] * hprev + sb_ref[pl.ds(r0, SUBLANES), :]
        sb_ref[pl.ds(r0, SUBLANES), :] = hg
        return hg[SUBLANES - 1:SUBLANES, :]

    hlast = lax.fori_loop(0, rows // SUBLANES, group_step, hcar_ref[...])
    hcar_ref[...] = hlast
    lru = sb_ref[...]
    mls_ref[:, 0:LRU_WIDTH] = (lru * _silu(gr)).astype(BF16)

    o0 = 2 * LRU_WIDTH
    bc = _dot(h, wrest_ref[:, o0:o0 + SC_WIDTH])
    cc = _dot(h, wrest_ref[:, o0 + SC_WIDTH:o0 + 2 * SC_WIDTH])
    hc = _dot(h, wrest_ref[:, o0 + 2 * SC_WIDTH:o0 + 3 * SC_WIDTH])
    gc = _dot(h, wrest_ref[:, o0 + 3 * SC_WIDTH:o0 + 4 * SC_WIDTH])
    u = cc * hc
    ext_sc[SUBLANES:SUBLANES + rows, :] = u
    cu = scw_ref[SC_CONV - 1:SC_CONV, :] * u
    for j in range(SC_CONV - 1):
        o = SUBLANES - (SC_CONV - 1) + j
        cu = cu + scw_ref[j:j + 1, :] * ext_sc[o:o + rows, :]
    tail_sc = ext_sc[rows:rows + SUBLANES, :]
    ext_sc[0:SUBLANES, :] = tail_sc
    mls_ref[:, LRU_WIDTH:LRU_WIDTH + SC_WIDTH] = (bc * cu * _silu(gc)).astype(BF16)

    @pl.when(t == last_t)
    def _():
        rgh_ref[...] = hlast
        rgc_ref[...] = tail[SUBLANES - (LRU_CONV - 1):SUBLANES, :]
        scc_ref[...] = tail_sc[SUBLANES - (SC_CONV - 1):SUBLANES, :]


def _const_spec(shape):
    nd = len(shape)
    return pl.BlockSpec(shape, lambda *_: (0,) * nd, pipeline_mode=pl.Buffered(1))


def _prompt_in(x, lw, sel):
    b, s, d = x.shape
    rows = PROJ_ROWS
    assert rows == ATTN_BLOCK and s % rows == 0
    nt = s // rows
    row_spec = lambda w: pl.BlockSpec((None, rows, w), lambda bi, ti: (bi, ti, 0))
    last_spec = lambda r, w: pl.BlockSpec((None, r, w), lambda bi, ti: (bi, 0, 0))
    weights = (lw["g"], lw["w_qkvg"], lw["w_rest"], lw["w_fl"], lw["b_f"], sel, lw["rg_conv_w"],
               lw["rg_conv_b"], lw["wa"], lw["ba"], lw["wx"], lw["bx"], lw["lam"], lw["sc_conv_w"])
    out_shape = (
        jax.ShapeDtypeStruct((b, s, N_HEADS * AUG_DIM), BF16),
        jax.ShapeDtypeStruct((b, s, N_HEADS * AUG_DIM), BF16),
        jax.ShapeDtypeStruct((b, s, ATT_WIDTH), F32),
        jax.ShapeDtypeStruct((b, s, ATT_WIDTH), F32),
        jax.ShapeDtypeStruct((b, nt, ATT_WIDTH, rows), BF16),
        jax.ShapeDtypeStruct((b, s, ATT_WIDTH), BF16),
        jax.ShapeDtypeStruct((b, s, LRU_WIDTH + SC_WIDTH), BF16),
        jax.ShapeDtypeStruct((b, s, N_HEADS), F32),
        jax.ShapeDtypeStruct((b, 1, LRU_WIDTH), F32),
        jax.ShapeDtypeStruct((b, LRU_CONV - 1, LRU_WIDTH), F32),
        jax.ShapeDtypeStruct((b, SC_CONV - 1, SC_WIDTH), F32),
    )
    out_specs = (
        row_spec(N_HEADS * AUG_DIM), row_spec(N_HEADS * AUG_DIM),
        row_spec(ATT_WIDTH), row_spec(ATT_WIDTH),
        pl.BlockSpec((None, None, ATT_WIDTH, rows), lambda bi, ti: (bi, ti, 0, 0)),
        row_spec(ATT_WIDTH), row_spec(LRU_WIDTH + SC_WIDTH), row_spec(N_HEADS),
        last_spec(1, LRU_WIDTH), last_spec(LRU_CONV - 1, LRU_WIDTH), last_spec(SC_CONV - 1, SC_WIDTH),
    )
    return pl.pallas_call(
        functools.partial(_prompt_in_kernel, rows=rows),
        out_shape=out_shape,
        grid=(b, nt),
        in_specs=[row_spec(d)] + [_const_spec(w.shape) for w in weights],
        out_specs=out_specs,
        scratch_shapes=[
            pltpu.VMEM((rows + SUBLANES, LRU_WIDTH), F32),
            pltpu.VMEM((rows + SUBLANES, SC_WIDTH), F32),
            pltpu.VMEM((rows, LRU_WIDTH), F32),
            pltpu.VMEM((rows, LRU_WIDTH), F32),
            pltpu.VMEM((1, LRU_WIDTH), F32),
            pltpu.VMEM((1, LANES), F32),
        ],
        compiler_params=pltpu.CompilerParams(
            dimension_semantics=("arbitrary", "arbitrary"), vmem_limit_bytes=VMEM_LIMIT_BYTES),
        name="prompt_in",
    )(x, *weights)


def _prompt_attn_kernel(qp_ref, kp_ref, vt_ref, gsa_ref, o_ref, m_ref, l_ref, acc_ref):
    qi = pl.program_id(1)
    blk = ATTN_BLOCK
    m_ref[...] = jnp.full(m_ref.shape, -jnp.inf, F32)
    l_ref[...] = jnp.zeros(l_ref.shape, F32)
    acc_ref[...] = jnp.zeros(acc_ref.shape, F32)

    def kv_block(j, masked):
        r0 = pl.multiple_of(j * blk, blk)
        for hd in range(N_HEADS):
            kblk = kp_ref[pl.ds(r0, blk), hd * AUG_DIM:(hd + 1) * AUG_DIM]
            s = _dot_nt(kblk, qp_ref[:, hd * AUG_DIM:(hd + 1) * AUG_DIM])
            if masked:
                krow = lax.broadcasted_iota(jnp.int32, (blk, blk), 0)
                qcol = lax.broadcasted_iota(jnp.int32, (blk, blk), 1)
                s = jnp.where(krow <= qcol, s, -jnp.inf)
            m_old = m_ref[hd]
            m_new = jnp.maximum(m_old, jnp.max(s, axis=0, keepdims=True))
            alpha = jnp.exp2(m_old - m_new)
            p = jnp.exp2(s - m_new)
            l_ref[hd] = alpha * l_ref[hd] + jnp.sum(p, axis=0, keepdims=True)
            m_ref[hd] = m_new
            vt = vt_ref[j, hd * HEAD_DIM:(hd + 1) * HEAD_DIM, :]
            acc_ref[hd] = alpha * acc_ref[hd] + _dot(vt, p.astype(BF16))

    def body(j, carry):
        kv_block(j, False)
        return carry

    lax.fori_loop(0, qi, body, 0)
    kv_block(qi, True)
    for hd in range(N_HEADS):
        lo, hi = hd * HEAD_DIM, (hd + 1) * HEAD_DIM
        att = (acc_ref[hd] / l_ref[hd]).T
        o_ref[:, lo:hi] = (att * gsa_ref[:, lo:hi].astype(F32)).astype(BF16)


def _prompt_attn(qp, kp, vt, gsa):
    b, s, _ = qp.shape
    blk = ATTN_BLOCK
    nq = s // blk
    qspec = pl.BlockSpec((None, blk, N_HEADS * AUG_DIM), lambda bi, qi: (bi, qi, 0))
    ospec = pl.BlockSpec((None, blk, ATT_WIDTH), lambda bi, qi: (bi, qi, 0))
    return pl.pallas_call(
        _prompt_attn_kernel,
        out_shape=jax.ShapeDtypeStruct((b, s, ATT_WIDTH), BF16),
        grid=(b, nq),
        in_specs=[qspec,
                  pl.BlockSpec((None, s, N_HEADS * AUG_DIM), lambda bi, qi: (bi, 0, 0)),
                  pl.BlockSpec((None, nq, ATT_WIDTH, blk), lambda bi, qi: (bi, 0, 0, 0)),
                  ospec],
        out_specs=ospec,
        scratch_shapes=[
            pltpu.VMEM((N_HEADS, 1, blk), F32),
            pltpu.VMEM((N_HEADS, 1, blk), F32),
            pltpu.VMEM((N_HEADS, HEAD_DIM, blk), F32),
        ],
        compiler_params=pltpu.CompilerParams(
            dimension_semantics=("arbitrary", "arbitrary"), vmem_limit_bytes=VMEM_LIMIT_BYTES),
        name="prompt_attn",
    )(qp, kp, vt, gsa)


def _out_kernel(x_ref, ma_ref, mls_ref, wo_ref, fg_ref, y_ref, *, final_norm):
    y = x_ref[...] + _dot(ma_ref[...].astype(BF16), wo_ref[0:ATT_WIDTH, :])
    y = y + _dot(mls_ref[...], wo_ref[ATT_WIDTH:, :])
    if final_norm:
        y = y * _rms_scale(y) * fg_ref[...]
    y_ref[...] = y


def _out_proj(x2, ma2, mls2, wo, fg, *, rows, final_norm):
    n, d = x2.shape
    spec = lambda w: pl.BlockSpec((rows, w), lambda i: (i, 0))
    return pl.pallas_call(
        functools.partial(_out_kernel, final_norm=final_norm),
        out_shape=jax.ShapeDtypeStruct((n, d), F32),
        grid=(n // rows,),
        in_specs=[spec(d), spec(ATT_WIDTH), spec(LRU_WIDTH + SC_WIDTH),
                  _const_spec(wo.shape), _const_spec(fg.shape)],
        out_specs=spec(d),
        compiler_params=pltpu.CompilerParams(
            dimension_semantics=("arbitrary",), vmem_limit_bytes=VMEM_LIMIT_BYTES),
        name="out_proj",
    )(x2, ma2, mls2, wo, fg)


def _sample_in_kernel(x_ref, g_ref, wqkvg_ref, wrest_ref, wfl_ref, bf_ref, cw_ref, cb_ref,
                      wa_ref, ba_ref, wx_ref, bx_ref, lam_ref, scw_ref, h0_ref, rgbuf_ref, scbuf_ref,
                      q_ref, k_ref, v_ref, gsa_ref, mls_ref, logf_ref, rgh_ref, rgc_ref, scc_ref):
    x = x_ref[...]
    h = (x * _rms_scale(x) * g_ref[...]).astype(BF16)
    q_ref[...] = _dot(h, wqkvg_ref[:, 0:ATT_WIDTH]) * (HEAD_DIM ** -0.5)
    k_ref[...] = _dot(h, wqkvg_ref[:, ATT_WIDTH:2 * ATT_WIDTH])
    v_ref[...] = _dot(h, wqkvg_ref[:, 2 * ATT_WIDTH:3 * ATT_WIDTH])
    gsa_ref[...] = _silu(_dot(h, wqkvg_ref[:, 3 * ATT_WIDTH:4 * ATT_WIDTH]))
    logf = _log_sigmoid(_dot(h, wfl_ref[...]) + bf_ref[...])
    logf_ref[...] = logf[:, 0:N_HEADS]

    xr = _dot(h, wrest_ref[:, 0:LRU_WIDTH])
    gr = _dot(h, wrest_ref[:, LRU_WIDTH:2 * LRU_WIDTH])
    xc = cb_ref[...] + cw_ref[LRU_CONV - 1:LRU_CONV, :] * xr
    for j in range(LRU_CONV - 1):
        xc = xc + cw_ref[j:j + 1, :] * rgbuf_ref[j]
    for j in range(LRU_CONV - 2):
        rgc_ref[j] = rgbuf_ref[j + 1]
    rgc_ref[LRU_CONV - 2] = xr
    a, drive = _lru_gates(xc, wa_ref, ba_ref, wx_ref, bx_ref, lam_ref)
    hn = a * h0_ref[...] + drive
    rgh_ref[...] = hn
    mls_ref[:, 0:LRU_WIDTH] = (hn * _silu(gr)).astype(BF16)

    o0 = 2 * LRU_WIDTH
    bc = _dot(h, wrest_ref[:, o0:o0 + SC_WIDTH])
    cc = _dot(h, wrest_ref[:, o0 + SC_WIDTH:o0 + 2 * SC_WIDTH])
    hc = _dot(h, wrest_ref[:, o0 + 2 * SC_WIDTH:o0 + 3 * SC_WIDTH])
    gc = _dot(h, wrest_ref[:, o0 + 3 * SC_WIDTH:o0 + 4 * SC_WIDTH])
    u = cc * hc
    cu = scw_ref[SC_CONV - 1:SC_CONV, :] * u
    for j in range(SC_CONV - 1):
        cu = cu + scw_ref[j:j + 1, :] * scbuf_ref[j]
    for j in range(SC_CONV - 2):
        scc_ref[j] = scbuf_ref[j + 1]
    scc_ref[SC_CONV - 2] = u
    mls_ref[:, LRU_WIDTH:LRU_WIDTH + SC_WIDTH] = (bc * cu * _silu(gc)).astype(BF16)


def _sample_in(x2, lw, h0, rgbuf_t, scbuf_t):
    n, _ = x2.shape
    weights = (lw["g"], lw["w_qkvg"], lw["w_rest"], lw["w_fl"], lw["b_f"], lw["rg_conv_w"],
               lw["rg_conv_b"], lw["wa"], lw["ba"], lw["wx"], lw["bx"], lw["lam"], lw["sc_conv_w"])
    out_shape = (
        jax.ShapeDtypeStruct((n, ATT_WIDTH), F32),
        jax.ShapeDtypeStruct((n, ATT_WIDTH), F32),
        jax.ShapeDtypeStruct((n, ATT_WIDTH), F32),
        jax.ShapeDtypeStruct((n, ATT_WIDTH), F32),
        jax.ShapeDtypeStruct((n, LRU_WIDTH + SC_WIDTH), BF16),
        jax.ShapeDtypeStruct((n, N_HEADS), F32),
        jax.ShapeDtypeStruct((n, LRU_WIDTH), F32),
        jax.ShapeDtypeStruct((LRU_CONV - 1, n, LRU_WIDTH), F32),
        jax.ShapeDtypeStruct((SC_CONV - 1, n, SC_WIDTH), F32),
    )
    operands = (x2,) + weights + (h0, rgbuf_t, scbuf_t)
    full = lambda shape: pl.BlockSpec(shape, lambda i, nd=len(shape): (0,) * nd)
    return pl.pallas_call(
        _sample_in_kernel,
        out_shape=out_shape,
        grid=(1,),
        in_specs=[_const_spec(o.shape) for o in operands],
        out_specs=tuple(full(o.shape) for o in out_shape),
        compiler_params=pltpu.CompilerParams(
            dimension_semantics=("arbitrary",), vmem_limit_bytes=VMEM_LIMIT_BYTES),
        name="sample_in",
    )(*operands)


def _decode_kernel(pt_ref, q_ref, kn_ref, vn_ref, fn_ref, gsa_ref, *refs):
    npg = DEC_PAGES
    k_refs = refs[0:npg]
    v_refs = refs[npg:2 * npg]
    f_refs = refs[2 * npg:3 * npg]
    o_ref = refs[3 * npg]
    qb_ref, m_ref, l_ref, acc_ref, tot_ref = refs[3 * npg + 1:]
    j = pl.program_id(1)
    page = LANES
    width = page * N_HEADS

    @pl.when(j == 0)
    def _():
        q = q_ref[...]
        qb_ref[...] = q.astype(BF16)
        m_ref[...] = jnp.sum(q * kn_ref[...], axis=1, keepdims=True)
        l_ref[...] = jnp.ones_like(l_ref)
        acc_ref[...] = vn_ref[...]
        tot_ref[...] = jnp.broadcast_to(fn_ref[...], (N_HEADS, LANES))

    ri = lax.broadcasted_iota(jnp.int32, (page, 2 * page), 0)
    ci = lax.broadcasted_iota(jnp.int32, (page, 2 * page), 1)
    u2 = jnp.where((ri > ci) | (ci >= page), 1.0, 0.0).astype(F32)
    rs = lax.broadcasted_iota(jnp.int32, (page, width), 0)
    cs = lax.broadcasted_iota(jnp.int32, (page, width), 1)
    spread = jnp.where(cs // N_HEADS == rs, 1.0, 0.0).astype(BF16)
    own = (lax.broadcasted_iota(jnp.int32, (N_HEADS, width), 1) % N_HEADS
           == lax.broadcasted_iota(jnp.int32, (N_HEADS, width), 0))

    tot = tot_ref[...]
    bias = []
    for i in range(npg):
        suf = jnp.dot(f_refs[i][...], u2, precision=lax.Precision.HIGHEST,
                      preferred_element_type=F32)
        bias.append(tot + suf[:, 0:page])
        tot = tot + suf[:, page:2 * page]
    tot_ref[...] = tot
    hi, mid, lo = _split3(jnp.concatenate(bias, axis=0))
    parts = jnp.concatenate([hi, mid, lo], axis=0).astype(BF16)
    spread_parts = _dot(parts, spread)
    rows = npg * N_HEADS
    bias_w = spread_parts[0:rows] + spread_parts[rows:2 * rows] + spread_parts[2 * rows:3 * rows]

    qb = qb_ref[...]
    scores = []
    for i in range(npg):
        kpage = k_refs[i][...].reshape(width, HEAD_DIM).astype(BF16)
        s = _dot_nt(qb, kpage) + bias_w[i * N_HEADS:(i + 1) * N_HEADS]
        scores.append(jnp.where(own, s, -jnp.inf))
    s_all = jnp.concatenate(scores, axis=1)
    m_old = m_ref[...]
    m_new = jnp.maximum(m_old, jnp.max(s_all, axis=1, keepdims=True))
    alpha = jnp.exp(m_old - m_new)
    p = jnp.exp(s_all - m_new)
    l_ref[...] = alpha * l_ref[...] + jnp.sum(p, axis=1, keepdims=True)
    m_ref[...] = m_new
    pb = p.astype(BF16)
    acc = alpha * acc_ref[...]
    for i in range(npg):
        vpage = v_refs[i][...].reshape(width, HEAD_DIM).astype(BF16)
        acc = acc + _dot(pb[:, i * width:(i + 1) * width], vpage)
    acc_ref[...] = acc

    @pl.when(j == pl.num_programs(1) - 1)
    def _():
        o_ref[...] = acc / l_ref[...] * gsa_ref[...]


def _decode_attn(layer, page_table, cache_k, cache_v, cache_logf_t, q3, kn3, vn3, fn3, gsa3):
    n, n_pages = page_table.shape
    page = cache_k.shape[2]
    assert page == LANES and n_pages % DEC_PAGES == 0
    tok = pl.BlockSpec((None, N_HEADS, HEAD_DIM), lambda bi, j, pt: (bi, 0, 0))

    def page_spec(i, tail):
        def imap(bi, j, pt):
            return (layer, pt[bi, n_pages - 1 - (j * DEC_PAGES + i)]) + (0,) * len(tail)
        return pl.BlockSpec((None, None) + tail, imap)

    kv_tail = (page, N_HEADS, HEAD_DIM)
    in_specs = ([tok, tok, tok, pl.BlockSpec((None, N_HEADS, 1), lambda bi, j, pt: (bi, 0, 0)), tok]
                + [page_spec(i, kv_tail) for i in range(DEC_PAGES)]
                + [page_spec(i, kv_tail) for i in range(DEC_PAGES)]
                + [page_spec(i, (N_HEADS, page)) for i in range(DEC_PAGES)])
    grid_spec = pltpu.PrefetchScalarGridSpec(
        num_scalar_prefetch=1,
        grid=(n, n_pages // DEC_PAGES),
        in_specs=in_specs,
        out_specs=tok,
        scratch_shapes=[
            pltpu.VMEM((N_HEADS, HEAD_DIM), BF16),
            pltpu.VMEM((N_HEADS, 1), F32),
            pltpu.VMEM((N_HEADS, 1), F32),
            pltpu.VMEM((N_HEADS, HEAD_DIM), F32),
            pltpu.VMEM((N_HEADS, LANES), F32),
        ],
    )
    return pl.pallas_call(
        _decode_kernel,
        out_shape=jax.ShapeDtypeStruct((n, N_HEADS, HEAD_DIM), F32),
        grid_spec=grid_spec,
        compiler_params=pltpu.CompilerParams(
            dimension_semantics=("arbitrary", "arbitrary"), vmem_limit_bytes=VMEM_LIMIT_BYTES),
        name="decode_attn",
    )(page_table, q3, kn3, vn3, fn3, gsa3,
      *([cache_k] * DEC_PAGES), *([cache_v] * DEC_PAGES), *([cache_logf_t] * DEC_PAGES))


def _block_diag(w):
    g, i, j = w.shape
    eye = jnp.eye(g, dtype=w.dtype)
    return (eye[:, None, :, None] * w[:, :, None, :]).reshape(g * i, g * j)


def _layer_weights(l, norm_g, w_in, b_f, rg_conv_w, rg_conv_b, rg_wa, rg_ba, rg_wx, rg_bx,
                   rg_lambda, sc_conv_w):
    w = w_in[l]
    row = lambda a: a.reshape(1, -1).astype(F32)
    return dict(
        g=row(norm_g[l]),
        w_qkvg=jnp.concatenate([w[:, 0:_OFF_FL], w[:, _OFF_GA:_OFF_XR]], axis=1).astype(BF16),
        w_rest=w[:, _OFF_XR:].astype(BF16),
        w_fl=jnp.pad(w[:, _OFF_FL:_OFF_GA], ((0, 0), (0, LANES - N_HEADS))).astype(BF16),
        b_f=jnp.pad(row(b_f[l]), ((0, 0), (0, LANES - N_HEADS))),
        rg_conv_w=rg_conv_w[l].astype(F32), rg_conv_b=row(rg_conv_b[l]),
        wa=_block_diag(rg_wa[l]).astype(BF16), ba=row(rg_ba[l]),
        wx=_block_diag(rg_wx[l]).astype(BF16), bx=row(rg_bx[l]),
        lam=row(rg_lambda[l]), sc_conv_w=sc_conv_w[l].astype(F32),
    )


def kernel(x_prompt, x_sample, cache_k, cache_v, cache_logf, state_rg_h, state_rg_conv, state_sc_conv, page_table, norm_g, w_in, b_f, rg_conv_w, rg_conv_b, rg_wa, rg_ba, rg_wx, rg_bx, rg_lambda, sc_conv_w, w_out, final_g):
    depth = w_in.shape[0]
    b, s, d = x_prompt.shape
    n = x_sample.shape[0]
    cache_logf_t = jnp.swapaxes(cache_logf, 2, 3)
    fg = final_g.reshape(1, d).astype(F32)
    sel = _bias_feature_matrix()

    yp = x_prompt
    ys = x_sample.reshape(n, d)
    sp, ss = [], []
    for l in range(depth):
        lw = _layer_weights(l, norm_g, w_in, b_f, rg_conv_w, rg_conv_b, rg_wa, rg_ba, rg_wx,
                            rg_bx, rg_lambda, sc_conv_w)
        wo = w_out[l].astype(BF16)
        last = l == depth - 1

        qp, kp, k, v, vt, gsa, mls, logf, rgh, rgc, scc = _prompt_in(yp, lw, sel)
        ma = _prompt_attn(qp, kp, vt, gsa)
        yp = _out_proj(yp.reshape(b * s, d), ma.reshape(b * s, ATT_WIDTH),
                       mls.reshape(b * s, LRU_WIDTH + SC_WIDTH), wo, fg,
                       rows=OUT_ROWS, final_norm=last).reshape(b, s, d)
        sp.append((k.reshape(b, s, N_HEADS, HEAD_DIM), v.reshape(b, s, N_HEADS, HEAD_DIM), logf,
                   rgh.reshape(b, LRU_WIDTH), rgc, scc))

        qs, ks, vs, gsas, mlss, logfs, rghs, rgcs, sccs = _sample_in(
            ys, lw, state_rg_h[l], jnp.swapaxes(state_rg_conv[l], 0, 1),
            jnp.swapaxes(state_sc_conv[l], 0, 1))
        heads = lambda a: a.reshape(n, N_HEADS, HEAD_DIM)
        mas = _decode_attn(l, page_table, cache_k, cache_v, cache_logf_t, heads(qs), heads(ks),
                           heads(vs), logfs.reshape(n, N_HEADS, 1), heads(gsas))
        ys = _out_proj(ys, mas.reshape(n, ATT_WIDTH), mlss, wo, fg, rows=n, final_norm=last)
        ss.append((ks.reshape(n, 1, N_HEADS, HEAD_DIM), vs.reshape(n, 1, N_HEADS, HEAD_DIM),
                   logfs.reshape(n, 1, N_HEADS), rghs, jnp.swapaxes(rgcs, 0, 1),
                   jnp.swapaxes(sccs, 0, 1)))

    stack = lambda states, i: jnp.stack([st[i] for st in states])
    return (yp, ys.reshape(n, 1, d),
            stack(sp, 0), stack(sp, 1), stack(sp, 2), stack(sp, 3), stack(sp, 4), stack(sp, 5),
            stack(ss, 0), stack(ss, 1), stack(ss, 2), stack(ss, 3), stack(ss, 4), stack(ss, 5))
```

```python
import functools

import numpy as np
import jax
import jax.numpy as jnp
from jax import lax
from jax.experimental import pallas as pl
from jax.experimental.pallas import tpu as pltpu

F32 = jnp.float32
BF16 = jnp.bfloat16

N_HEADS = 8
HEAD_DIM = 128
ATT_WIDTH = N_HEADS * HEAD_DIM
LRU_WIDTH = 512
LRU_BLOCKS = 8
LRU_CONV = 4
LRU_C = 8.0
SC_WIDTH = 512
SC_CONV = 3
EPS = 1e-6
_OFF_FL = 3 * ATT_WIDTH
_OFF_GA = _OFF_FL + N_HEADS
_OFF_XR = _OFF_GA + ATT_WIDTH

LANES = 128
SUBLANES = 8
VMEM_LIMIT_BYTES = 60 * 1024 * 1024

PROJ_ROWS = 512
ATTN_BLOCK = 512
OUT_ROWS = 512
DEC_PAGES = 16
PROJ_COLS = 2 * HEAD_DIM

LOG2E = 1.4426950408889634
AUG_DIM = 2 * HEAD_DIM
_SPLIT_LANES = (0, 16, 32)
_ONES_LANE = N_HEADS


def _sigmoid(x):
    return 1.0 / (1.0 + jnp.exp(-x))


def _silu(x):
    return x * _sigmoid(x)


def _log_sigmoid(x):
    return jnp.minimum(x, 0.0) - jnp.log1p(jnp.exp(-jnp.abs(x)))


def _softplus(x):
    return jnp.maximum(x, 0.0) + jnp.log1p(jnp.exp(-jnp.abs(x)))


def _rms_scale(x):
    return lax.rsqrt(jnp.mean(x * x, axis=-1, keepdims=True) + EPS)


def _dot(a, b):
    return jnp.dot(a, b, preferred_element_type=F32)


def _dot_nt(a, b):
    return lax.dot_general(a, b, (((1,), (1,)), ((), ())), preferred_element_type=F32)


def _split3(x):
    hi = x.astype(BF16).astype(F32)
    r = x - hi
    mid = r.astype(BF16).astype(F32)
    lo = (r - mid).astype(BF16).astype(F32)
    return hi, mid, lo


def _lru_gates(xc, wa, ba, wx, bx, lam):
    xb = xc.astype(BF16)
    r = _sigmoid(_dot(xb, wa) + ba)
    i = _sigmoid(_dot(xb, wx) + bx)
    a = jnp.exp((-LRU_C) * r * _softplus(-lam))
    drive = jnp.sqrt(1.0 - a * a) * (i * xc)
    return a, drive


def _bias_feature_matrix():
    m = np.zeros((LANES, 2 * ATT_WIDTH), np.float32)
    for h in range(N_HEADS):
        for part, lane0 in enumerate(_SPLIT_LANES):
            m[lane0 + h, h * HEAD_DIM + part] = 1.0
            m[lane0 + h, ATT_WIDTH + h * HEAD_DIM + 3 + part] = -1.0
            m[_ONES_LANE, h * HEAD_DIM + 3 + part] = 1.0
            m[_ONES_LANE, ATT_WIDTH + h * HEAD_DIM + part] = 1.0
    return jnp.asarray(m, BF16)


def _prompt_in_kernel(x_ref, g_ref, wqkvg_ref, wrest_ref, wfl_ref, bf_ref, sel_ref, cw_ref, cb_ref,
                      wa_ref, ba_ref, wx_ref, bx_ref, lam_ref, scw_ref, *refs, rows, n_state_in):
    (qp_ref, kp_ref, k_ref, v_ref, vt_ref, gsa_ref, mls_ref, logf_ref, rgh_ref, rgc_ref, scc_ref,
     ext_rg, ext_sc, hcar_ref, ccar_ref) = refs[n_state_in:]
    t = pl.program_id(1)
    last_t = pl.num_programs(1) - 1

    @pl.when(t == 0)
    def _():
        ext_rg[0:SUBLANES, :] = jnp.zeros((SUBLANES, LRU_WIDTH), F32)
        ext_sc[0:SUBLANES, :] = jnp.zeros((SUBLANES, SC_WIDTH), F32)
        hcar_ref[...] = jnp.zeros_like(hcar_ref)
        ccar_ref[...] = jnp.zeros_like(ccar_ref)

    x = x_ref[...]
    h = (x * _rms_scale(x) * g_ref[...]).astype(BF16)

    logf = _log_sigmoid(_dot(h, wfl_ref[...]) + bf_ref[...])
    logf_ref[...] = logf[:, 0:N_HEADS]
    row = lax.broadcasted_iota(jnp.int32, (rows, LANES), 0)
    lane = lax.broadcasted_iota(jnp.int32, (rows, LANES), 1)
    cs = logf
    shift = 1
    while shift < rows:
        cs = cs + jnp.where(row >= shift, pltpu.roll(cs, shift, 0), 0.0)
        shift *= 2
    cs = cs + ccar_ref[...]
    ccar_ref[...] = cs[rows - 1:rows, :]
    cx = jnp.where(lane < N_HEADS, cs * LOG2E, jnp.where(lane == _ONES_LANE, 1.0, 0.0))
    hi, mid, lo = _split3(cx)
    packed = (hi + pltpu.roll(mid, _SPLIT_LANES[1], 1)
              + pltpu.roll(lo, _SPLIT_LANES[2], 1)).astype(BF16)

    ch = PROJ_COLS
    heads_per_chunk = ch // HEAD_DIM

    def attn_chunk(c):
        lo_c, hi_c = c * ch, (c + 1) * ch
        q = _dot(h, wqkvg_ref[:, lo_c:hi_c]) * (HEAD_DIM ** -0.5 * LOG2E)
        k = _dot(h, wqkvg_ref[:, ATT_WIDTH + lo_c:ATT_WIDTH + hi_c])
        v = _dot(h, wqkvg_ref[:, 2 * ATT_WIDTH + lo_c:2 * ATT_WIDTH + hi_c])
        ga = _dot(h, wqkvg_ref[:, 3 * ATT_WIDTH + lo_c:3 * ATT_WIDTH + hi_c])
        augq = _dot(packed, sel_ref[:, lo_c:hi_c])
        augk = _dot(packed, sel_ref[:, ATT_WIDTH + lo_c:ATT_WIDTH + hi_c])
        vt_ref[lo_c:hi_c, :] = v.T.astype(BF16)
        gsa_ref[:, lo_c:hi_c] = _silu(ga).astype(BF16)
        for hh in range(heads_per_chunk):
            hd = c * heads_per_chunk + hh
            src = slice(hh * HEAD_DIM, (hh + 1) * HEAD_DIM)
            qp_ref[:, hd * AUG_DIM:hd * AUG_DIM + HEAD_DIM] = q[:, src].astype(BF16)
            qp_ref[:, hd * AUG_DIM + HEAD_DIM:(hd + 1) * AUG_DIM] = augq[:, src].astype(BF16)
            kp_ref[:, hd * AUG_DIM:hd * AUG_DIM + HEAD_DIM] = k[:, src].astype(BF16)
            kp_ref[:, hd * AUG_DIM + HEAD_DIM:(hd + 1) * AUG_DIM] = augk[:, src].astype(BF16)
        return k, v

    rmod =lax.broadcasted_iota(jnp.int32, (rows, ch), 0) % SUBLANES

    def lru_chunk(c):
        cols = slice(c * ch, (c + 1) * ch)
        xr = _dot(h, wrest_ref[:, c * ch:(c + 1) * ch])
        gr = _dot(h, wrest_ref[:, LRU_WIDTH + c * ch:LRU_WIDTH + (c + 1) * ch])
        ext_rg[SUBLANES:SUBLANES + rows, cols] = xr
        xc = cb_ref[:, cols] + cw_ref[LRU_CONV - 1:LRU_CONV, cols] * xr
        for j in range(LRU_CONV - 1):
            o = SUBLANES - (LRU_CONV - 1) + j
            xc = xc + cw_ref[j:j + 1, cols] * ext_rg[o:o + rows, cols]
        tail_c = ext_rg[rows:rows + SUBLANES, cols]
        ext_rg[0:SUBLANES, cols] = tail_c
        a, drive = _lru_gates(xc, wa_ref[cols, cols], ba_ref[:, cols], wx_ref[cols, cols],
                              bx_ref[:, cols], lam_ref[:, cols])
        shift = 1
        while shift < SUBLANES:
            keep = rmod >= shift
            a_sh = pltpu.roll(a, shift, 0)
            d_sh = pltpu.roll(drive, shift, 0)
            drive = jnp.where(keep, a * d_sh + drive, drive)
            a = jnp.where(keep, a * a_sh, a)
            shift *= 2
        hprev = hcar_ref[:, cols]
        groups = []
        for gi in range(rows // SUBLANES):
            sl = slice(gi * SUBLANES, (gi + 1) * SUBLANES)
            hg = a[sl, :] * hprev + drive[sl, :]
            groups.append(hg)
            hprev = hg[SUBLANES - 1:SUBLANES, :]
        hcar_ref[:, cols] = hprev
        lru = jnp.concatenate(groups, axis=0)
        mls_ref[:, cols] = (lru * _silu(gr)).astype(BF16)
        return hprev, tail_c

    def conv_chunk(c):
        cols = slice(c * ch, (c + 1) * ch)
        o0 = 2 * LRU_WIDTH + c * ch
        bc = _dot(h, wrest_ref[:, o0:o0 + ch])
        cc = _dot(h, wrest_ref[:, o0 + SC_WIDTH:o0 + SC_WIDTH + ch])
        hc = _dot(h, wrest_ref[:, o0 + 2 * SC_WIDTH:o0 + 2 * SC_WIDTH + ch])
        gc = _dot(h, wrest_ref[:, o0 + 3 * SC_WIDTH:o0 + 3 * SC_WIDTH + ch])
        u = cc * hc
        ext_sc[SUBLANES:SUBLANES + rows, cols] = u
        cu = scw_ref[SC_CONV - 1:SC_CONV, cols] * u
        for j in range(SC_CONV - 1):
            o = SUBLANES - (SC_CONV - 1) + j
            cu = cu + scw_ref[j:j + 1, cols] * ext_sc[o:o + rows, cols]
        tail_c = ext_sc[rows:rows + SUBLANES, cols]
        ext_sc[0:SUBLANES, cols] = tail_c
        mls_ref[:, LRU_WIDTH + c * ch:LRU_WIDTH + (c + 1) * ch] = (bc * cu * _silu(gc)).astype(BF16)
        return tail_c

    kv = [attn_chunk(c) for c in range(ATT_WIDTH // ch)]
    k_ref[...] = jnp.concatenate([c[0] for c in kv], axis=1).reshape(rows, N_HEADS, HEAD_DIM)
    v_ref[...] = jnp.concatenate([c[1] for c in kv], axis=1).reshape(rows, N_HEADS, HEAD_DIM)
    lru_out = [lru_chunk(c) for c in range(LRU_WIDTH // ch)]
    hlast = jnp.concatenate([o[0] for o in lru_out], axis=1)
    tail = jnp.concatenate([o[1] for o in lru_out], axis=1)
    tail_sc = jnp.concatenate([conv_chunk(c) for c in range(SC_WIDTH // ch)], axis=1)

    @pl.when(t == last_t)
    def _():
        rgh_ref[...] = hlast
        rgc_ref[...] = tail[SUBLANES - (LRU_CONV - 1):SUBLANES, :]
        scc_ref[...] = tail_sc[SUBLANES - (SC_CONV - 1):SUBLANES, :]


def _const_spec(shape):
    nd = len(shape)
    return pl.BlockSpec(shape, lambda *_: (0,) * nd, pipeline_mode=pl.Buffered(1))


def _prompt_in(x, lw, sel, layer, depth, kv_state):
    b, s, d = x.shape
    rows = PROJ_ROWS
    assert rows == ATTN_BLOCK and s % rows == 0
    nt = s // rows
    row_spec = lambda w: pl.BlockSpec((None, rows, w), lambda bi, ti: (bi, ti, 0))
    last_spec = lambda r, w: pl.BlockSpec((None, r, w), lambda bi, ti: (bi, 0, 0))
    state_spec = pl.BlockSpec((None, None, rows, N_HEADS, HEAD_DIM),
                              lambda bi, ti: (layer, bi, ti, 0, 0))
    weights = (lw["g"], lw["w_qkvg"], lw["w_rest"], lw["w_fl"], lw["b_f"], sel, lw["rg_conv_w"],
               lw["rg_conv_b"], lw["wa"], lw["ba"], lw["wx"], lw["bx"], lw["lam"], lw["sc_conv_w"])
    state_in = () if kv_state is None else tuple(kv_state)
    n_in = 1 + len(weights)
    aliases = {n_in + i: 2 + i for i in range(len(state_in))}
    out_shape = (
        jax.ShapeDtypeStruct((b, s, N_HEADS * AUG_DIM), BF16),
        jax.ShapeDtypeStruct((b, s, N_HEADS * AUG_DIM), BF16),
        jax.ShapeDtypeStruct((depth, b, s, N_HEADS, HEAD_DIM), F32),
        jax.ShapeDtypeStruct((depth, b, s, N_HEADS, HEAD_DIM), F32),
        jax.ShapeDtypeStruct((b, nt, ATT_WIDTH, rows), BF16),
        jax.ShapeDtypeStruct((b, s, ATT_WIDTH), BF16),
        jax.ShapeDtypeStruct((b, s, LRU_WIDTH + SC_WIDTH), BF16),
        jax.ShapeDtypeStruct((b, s, N_HEADS), F32),
        jax.ShapeDtypeStruct((b, 1, LRU_WIDTH), F32),
        jax.ShapeDtypeStruct((b, LRU_CONV - 1, LRU_WIDTH), F32),
        jax.ShapeDtypeStruct((b, SC_CONV - 1, SC_WIDTH), F32),
    )
    out_specs = (
        row_spec(N_HEADS * AUG_DIM), row_spec(N_HEADS * AUG_DIM),
        state_spec, state_spec,
        pl.BlockSpec((None, None, ATT_WIDTH, rows), lambda bi, ti: (bi, ti, 0, 0)),
        row_spec(ATT_WIDTH), row_spec(LRU_WIDTH + SC_WIDTH), row_spec(N_HEADS),
        last_spec(1, LRU_WIDTH), last_spec(LRU_CONV - 1, LRU_WIDTH), last_spec(SC_CONV - 1, SC_WIDTH),
    )
    return pl.pallas_call(
        functools.partial(_prompt_in_kernel, rows=rows, n_state_in=len(state_in)),
        out_shape=out_shape,
        grid=(b, nt),
        in_specs=([row_spec(d)] + [_const_spec(w.shape) for w in weights]
                  + [pl.BlockSpec(memory_space=pl.ANY)] * len(state_in)),
        out_specs=out_specs,
        scratch_shapes=[
            pltpu.VMEM((rows + SUBLANES, LRU_WIDTH), F32),
            pltpu.VMEM((rows + SUBLANES, SC_WIDTH), F32),
            pltpu.VMEM((1, LRU_WIDTH), F32),
            pltpu.VMEM((1, LANES), F32),
        ],
        input_output_aliases=aliases,
        compiler_params=pltpu.CompilerParams(
            dimension_semantics=("arbitrary", "arbitrary"), vmem_limit_bytes=VMEM_LIMIT_BYTES),
        name="prompt_in",
    )(x, *weights, *state_in)


def _prompt_attn_kernel(qp_ref, kp_ref, vt_ref, gsa_ref, o_ref, m_ref, l_ref, acc_ref):
    qi = pl.program_id(1)
    blk = ATTN_BLOCK
    m_ref[...] = jnp.full(m_ref.shape, -jnp.inf, F32)
    l_ref[...] = jnp.zeros(l_ref.shape, F32)
    acc_ref[...] = jnp.zeros(acc_ref.shape, F32)

    def kv_block(j, masked):
        r0 = pl.multiple_of(j * blk, blk)
        for hd in range(N_HEADS):
            kblk = kp_ref[pl.ds(r0, blk), hd * AUG_DIM:(hd + 1) * AUG_DIM]
            s = _dot_nt(kblk, qp_ref[:, hd * AUG_DIM:(hd + 1) * AUG_DIM])
            if masked:
                krow = lax.broadcasted_iota(jnp.int32, (blk, blk), 0)
                qcol = lax.broadcasted_iota(jnp.int32, (blk, blk), 1)
                s = jnp.where(krow <= qcol, s, -jnp.inf)
            m_old = m_ref[hd]
            m_new = jnp.maximum(m_old, jnp.max(s, axis=0, keepdims=True))
            alpha = jnp.exp2(m_old - m_new)
            p = jnp.exp2(s - m_new)
            l_ref[hd] = alpha * l_ref[hd] + jnp.sum(p, axis=0, keepdims=True)
            m_ref[hd] = m_new
            vt = vt_ref[j, hd * HEAD_DIM:(hd + 1) * HEAD_DIM, :]
            acc_ref[hd] = alpha * acc_ref[hd] + _dot(vt, p.astype(BF16))

    def body(j, carry):
        kv_block(j, False)
        return carry

    lax.fori_loop(0, qi, body, 0)
    kv_block(qi, True)
    for hd in range(N_HEADS):
        lo, hi = hd * HEAD_DIM, (hd + 1) * HEAD_DIM
        att = (acc_ref[hd] / l_ref[hd]).T
        o_ref[:, lo:hi] = (att * gsa_ref[:, lo:hi].astype(F32)).astype(BF16)


def _prompt_attn(qp, kp, vt, gsa):
    b, s, _ = qp.shape
    blk = ATTN_BLOCK
    nq = s // blk
    qspec = pl.BlockSpec((None, blk, N_HEADS * AUG_DIM), lambda bi, qi: (bi, qi, 0))
    ospec = pl.BlockSpec((None, blk, ATT_WIDTH), lambda bi, qi: (bi, qi, 0))
    return pl.pallas_call(
        _prompt_attn_kernel,
        out_shape=jax.ShapeDtypeStruct((b, s, ATT_WIDTH), BF16),
        grid=(b, nq),
        in_specs=[qspec,
                  pl.BlockSpec((None, s, N_HEADS * AUG_DIM), lambda bi, qi: (bi, 0, 0)),
                  pl.BlockSpec((None, nq, ATT_WIDTH, blk), lambda bi, qi: (bi, 0, 0, 0)),
                  ospec],
        out_specs=ospec,
        scratch_shapes=[
            pltpu.VMEM((N_HEADS, 1, blk), F32),
            pltpu.VMEM((N_HEADS, 1, blk), F32),
            pltpu.VMEM((N_HEADS, HEAD_DIM, blk), F32),
        ],
        compiler_params=pltpu.CompilerParams(
            dimension_semantics=("arbitrary", "arbitrary"), vmem_limit_bytes=VMEM_LIMIT_BYTES),
        name="prompt_attn",
    )(qp, kp, vt, gsa)


def _out_kernel(x_ref, ma_ref, mls_ref, wo_ref, fg_ref, y_ref, *, final_norm):
    y = x_ref[...] + _dot(ma_ref[...].astype(BF16), wo_ref[0:ATT_WIDTH, :])
    y = y + _dot(mls_ref[...], wo_ref[ATT_WIDTH:, :])
    if final_norm:
        y = y * _rms_scale(y) * fg_ref[...]
    y_ref[...] = y


def _out_proj(x2, ma2, mls2, wo, fg, *, rows, final_norm):
    n, d = x2.shape
    spec = lambda w: pl.BlockSpec((rows, w), lambda i: (i, 0))
    return pl.pallas_call(
        functools.partial(_out_kernel, final_norm=final_norm),
        out_shape=jax.ShapeDtypeStruct((n, d), F32),
        grid=(n // rows,),
        in_specs=[spec(d), spec(ATT_WIDTH), spec(LRU_WIDTH + SC_WIDTH),
                  _const_spec(wo.shape), _const_spec(fg.shape)],
        out_specs=spec(d),
        compiler_params=pltpu.CompilerParams(
            dimension_semantics=("arbitrary",), vmem_limit_bytes=VMEM_LIMIT_BYTES),
        name="out_proj",
    )(x2, ma2, mls2, wo, fg)


def _sample_in_kernel(x_ref, g_ref, wqkvg_ref, wrest_ref, wfl_ref, bf_ref, cw_ref, cb_ref,
                      wa_ref, ba_ref, wx_ref, bx_ref, lam_ref, scw_ref, h0_ref, rgbuf_ref, scbuf_ref,
                      q_ref, k_ref, v_ref, gsa_ref, mls_ref, logf_ref, rgh_ref, rgc_ref, scc_ref):
    x = x_ref[...]
    h = (x * _rms_scale(x) * g_ref[...]).astype(BF16)
    q_ref[...] = _dot(h, wqkvg_ref[:, 0:ATT_WIDTH]) * (HEAD_DIM ** -0.5)
    k_ref[...] = _dot(h, wqkvg_ref[:, ATT_WIDTH:2 * ATT_WIDTH])
    v_ref[...] = _dot(h, wqkvg_ref[:, 2 * ATT_WIDTH:3 * ATT_WIDTH])
    gsa_ref[...] = _silu(_dot(h, wqkvg_ref[:, 3 * ATT_WIDTH:4 * ATT_WIDTH]))
    logf = _log_sigmoid(_dot(h, wfl_ref[...]) + bf_ref[...])
    logf_ref[...] = logf[:, 0:N_HEADS]

    xr = _dot(h, wrest_ref[:, 0:LRU_WIDTH])
    gr = _dot(h, wrest_ref[:, LRU_WIDTH:2 * LRU_WIDTH])
    xc = cb_ref[...] + cw_ref[LRU_CONV - 1:LRU_CONV, :] * xr
    for j in range(LRU_CONV - 1):
        xc = xc + cw_ref[j:j + 1, :] * rgbuf_ref[j]
    for j in range(LRU_CONV - 2):
        rgc_ref[j] = rgbuf_ref[j + 1]
    rgc_ref[LRU_CONV - 2] = xr
    a, drive = _lru_gates(xc, wa_ref[...], ba_ref[...], wx_ref[...], bx_ref[...], lam_ref[...])
    hn = a * h0_ref[...] + drive
    rgh_ref[...] = hn
    mls_ref[:, 0:LRU_WIDTH] = (hn * _silu(gr)).astype(BF16)

    o0 = 2 * LRU_WIDTH
    bc = _dot(h, wrest_ref[:, o0:o0 + SC_WIDTH])
    cc = _dot(h, wrest_ref[:, o0 + SC_WIDTH:o0 + 2 * SC_WIDTH])
    hc = _dot(h, wrest_ref[:, o0 + 2 * SC_WIDTH:o0 + 3 * SC_WIDTH])
    gc = _dot(h, wrest_ref[:, o0 + 3 * SC_WIDTH:o0 + 4 * SC_WIDTH])
    u = cc * hc
    cu = scw_ref[SC_CONV - 1:SC_CONV, :] * u
    for j in range(SC_CONV - 1):
        cu = cu + scw_ref[j:j + 1, :] * scbuf_ref[j]
    for j in range(SC_CONV - 2):
        scc_ref[j] = scbuf_ref[j + 1]
    scc_ref[SC_CONV - 2] = u
    mls_ref[:, LRU_WIDTH:LRU_WIDTH + SC_WIDTH] = (bc * cu * _silu(gc)).astype(BF16)


def _sample_in(x2, lw, h0, rgbuf_t, scbuf_t):
    n, _ = x2.shape
    weights = (lw["g"], lw["w_qkvg"], lw["w_rest"], lw["w_fl"], lw["b_f"], lw["rg_conv_w"],
               lw["rg_conv_b"], lw["wa"], lw["ba"], lw["wx"], lw["bx"], lw["lam"], lw["sc_conv_w"])
    out_shape = (
        jax.ShapeDtypeStruct((n, ATT_WIDTH), F32),
        jax.ShapeDtypeStruct((n, ATT_WIDTH), F32),
        jax.ShapeDtypeStruct((n, ATT_WIDTH), F32),
        jax.ShapeDtypeStruct((n, ATT_WIDTH), F32),
        jax.ShapeDtypeStruct((n, LRU_WIDTH + SC_WIDTH), BF16),
        jax.ShapeDtypeStruct((n, N_HEADS), F32),
        jax.ShapeDtypeStruct((n, LRU_WIDTH), F32),
        jax.ShapeDtypeStruct((LRU_CONV - 1, n, LRU_WIDTH), F32),
        jax.ShapeDtypeStruct((SC_CONV - 1, n, SC_WIDTH), F32),
    )
    operands = (x2,) + weights + (h0, rgbuf_t, scbuf_t)
    full = lambda shape: pl.BlockSpec(shape, lambda i, nd=len(shape): (0,) * nd)
    return pl.pallas_call(
        _sample_in_kernel,
        out_shape=out_shape,
        grid=(1,),
        in_specs=[_const_spec(o.shape) for o in operands],
        out_specs=tuple(full(o.shape) for o in out_shape),
        compiler_params=pltpu.CompilerParams(
            dimension_semantics=("arbitrary",), vmem_limit_bytes=VMEM_LIMIT_BYTES),
        name="sample_in",
    )(*operands)


def _decode_kernel(pt_ref, q_ref, kn_ref, vn_ref, fn_ref, gsa_ref, *refs):
    npg = DEC_PAGES
    k_refs = refs[0:npg]
    v_refs = refs[npg:2 * npg]
    f_refs = refs[2 * npg:3 * npg]
    o_ref = refs[3 * npg]
    qb_ref, m_ref, l_ref, acc_ref, tot_ref = refs[3 * npg + 1:]
    j = pl.program_id(1)
    page = LANES
    width = page * N_HEADS

    @pl.when(j == 0)
    def _():
        q = q_ref[...]
        qb_ref[...] = q.astype(BF16)
        m_ref[...] = jnp.sum(q * kn_ref[...], axis=1, keepdims=True)
        l_ref[...] = jnp.ones_like(l_ref)
        acc_ref[...] = vn_ref[...]
        tot_ref[...] = jnp.broadcast_to(fn_ref[...], (N_HEADS, LANES))

    ri = lax.broadcasted_iota(jnp.int32, (page, 2 * page), 0)
    ci = lax.broadcasted_iota(jnp.int32, (page, 2 * page), 1)
    u2 = jnp.where((ri > ci) | (ci >= page), 1.0, 0.0).astype(F32)
    rs = lax.broadcasted_iota(jnp.int32, (page, width), 0)
    cs = lax.broadcasted_iota(jnp.int32, (page, width), 1)
    spread = jnp.where(cs // N_HEADS == rs, 1.0, 0.0).astype(BF16)
    own = (lax.broadcasted_iota(jnp.int32, (N_HEADS, width), 1) % N_HEADS
           == lax.broadcasted_iota(jnp.int32, (N_HEADS, width), 0))

    tot = tot_ref[...]
    bias = []
    for i in range(npg):
        suf = jnp.dot(f_refs[i][...], u2, precision=lax.Precision.HIGHEST,
                      preferred_element_type=F32)
        bias.append(tot + suf[:, 0:page])
        tot = tot + suf[:, page:2 * page]
    tot_ref[...] = tot
    hi, mid, lo = _split3(jnp.concatenate(bias, axis=0))
    parts = jnp.concatenate([hi, mid, lo], axis=0).astype(BF16)
    spread_parts = _dot(parts, spread)
    rows = npg * N_HEADS
    bias_w = spread_parts[0:rows] + spread_parts[rows:2 * rows] + spread_parts[2 * rows:3 * rows]

    qb = qb_ref[...]
    scores = []
    for i in range(npg):
        kpage = k_refs[i][...].reshape(width, HEAD_DIM).astype(BF16)
        s = _dot_nt(qb, kpage) + bias_w[i * N_HEADS:(i + 1) * N_HEADS]
        scores.append(jnp.where(own, s, -jnp.inf))
    s_all = jnp.concatenate(scores, axis=1)
    m_old = m_ref[...]
    m_new = jnp.maximum(m_old, jnp.max(s_all, axis=1, keepdims=True))
    alpha = jnp.exp(m_old - m_new)
    p = jnp.exp(s_all - m_new)
    l_ref[...] = alpha * l_ref[...] + jnp.sum(p, axis=1, keepdims=True)
    m_ref[...] = m_new
    pb = p.astype(BF16)
    acc = alpha * acc_ref[...]
    for i in range(npg):
        vpage = v_refs[i][...].reshape(width, HEAD_DIM).astype(BF16)
        acc = acc + _dot(pb[:, i * width:(i + 1) * width], vpage)
    acc_ref[...] = acc

    @pl.when(j == pl.num_programs(1) - 1)
    def _():
        o_ref[...] = acc / l_ref[...] * gsa_ref[...]


def _decode_attn(layer, page_table, cache_k, cache_v, cache_logf_t, q3, kn3, vn3, fn3, gsa3):
    n, n_pages = page_table.shape
    page = cache_k.shape[2]
    assert page == LANES and n_pages % DEC_PAGES == 0
    tok = pl.BlockSpec((None, N_HEADS, HEAD_DIM), lambda bi, j, pt: (bi, 0, 0))

    def page_spec(i, tail):
        def imap(bi, j, pt):
            return (layer, pt[bi, n_pages - 1 - (j * DEC_PAGES + i)]) + (0,) * len(tail)
        return pl.BlockSpec((None, None) + tail, imap)

    kv_tail = (page, N_HEADS, HEAD_DIM)
    in_specs = ([tok, tok, tok, pl.BlockSpec((None, N_HEADS, 1), lambda bi, j, pt: (bi, 0, 0)), tok]
                + [page_spec(i, kv_tail) for i in range(DEC_PAGES)]
                + [page_spec(i, kv_tail) for i in range(DEC_PAGES)]
                + [page_spec(i, (N_HEADS, page)) for i in range(DEC_PAGES)])
    grid_spec = pltpu.PrefetchScalarGridSpec(
        num_scalar_prefetch=1,
        grid=(n, n_pages // DEC_PAGES),
        in_specs=in_specs,
        out_specs=tok,
        scratch_shapes=[
            pltpu.VMEM((N_HEADS, HEAD_DIM), BF16),
            pltpu.VMEM((N_HEADS, 1), F32),
            pltpu.VMEM((N_HEADS, 1), F32),
            pltpu.VMEM((N_HEADS, HEAD_DIM), F32),
            pltpu.VMEM((N_HEADS, LANES), F32),
        ],
    )
    return pl.pallas_call(
        _decode_kernel,
        out_shape=jax.ShapeDtypeStruct((n, N_HEADS, HEAD_DIM), F32),
        grid_spec=grid_spec,
        compiler_params=pltpu.CompilerParams(
            dimension_semantics=("arbitrary", "arbitrary"), vmem_limit_bytes=VMEM_LIMIT_BYTES),
        name="decode_attn",
    )(page_table, q3, kn3, vn3, fn3, gsa3,
      *([cache_k] * DEC_PAGES), *([cache_v] * DEC_PAGES), *([cache_logf_t] * DEC_PAGES))


def _block_diag(w):
    g, i, j = w.shape
    eye = jnp.eye(g, dtype=w.dtype)
    return (eye[:, None, :, None] * w[:, :, None, :]).reshape(g * i, g * j)


def _layer_weights(l, norm_g, w_in, b_f, rg_conv_w, rg_conv_b, rg_wa, rg_ba, rg_wx, rg_bx,
                   rg_lambda, sc_conv_w):
    w = w_in[l]
    row = lambda a: a.reshape(1, -1).astype(F32)
    return dict(
        g=row(norm_g[l]),
        w_qkvg=jnp.concatenate([w[:, 0:_OFF_FL], w[:, _OFF_GA:_OFF_XR]], axis=1).astype(BF16),
        w_rest=w[:, _OFF_XR:].astype(BF16),
        w_fl=jnp.pad(w[:, _OFF_FL:_OFF_GA], ((0, 0), (0, LANES - N_HEADS))).astype(BF16),
        b_f=jnp.pad(row(b_f[l]), ((0, 0), (0, LANES - N_HEADS))),
        rg_conv_w=rg_conv_w[l].astype(F32), rg_conv_b=row(rg_conv_b[l]),
        wa=_block_diag(rg_wa[l]).astype(BF16), ba=row(rg_ba[l]),
        wx=_block_diag(rg_wx[l]).astype(BF16), bx=row(rg_bx[l]),
        lam=row(rg_lambda[l]), sc_conv_w=sc_conv_w[l].astype(F32),
    )


def kernel(x_prompt, x_sample, cache_k, cache_v, cache_logf, state_rg_h, state_rg_conv, state_sc_conv, page_table, norm_g, w_in, b_f, rg_conv_w, rg_conv_b, rg_wa, rg_ba, rg_wx, rg_bx, rg_lambda, sc_conv_w, w_out, final_g):
    depth = w_in.shape[0]
    b, s, d = x_prompt.shape
    n = x_sample.shape[0]
    cache_logf_t = jnp.swapaxes(cache_logf, 2, 3)
    fg = final_g.reshape(1, d).astype(F32)
    sel = _bias_feature_matrix()

    yp = x_prompt
    ys = x_sample.reshape(n, d)
    sp, ss = [], []
    kv_state = None
    for l in range(depth):
        lw = _layer_weights(l, norm_g, w_in, b_f, rg_conv_w, rg_conv_b, rg_wa, rg_ba, rg_wx,
                            rg_bx, rg_lambda, sc_conv_w)
        wo = w_out[l].astype(BF16)
        last = l == depth - 1

        qp, kp, k_all, v_all, vt, gsa, mls, logf, rgh, rgc, scc = _prompt_in(
            yp, lw, sel, l, depth, kv_state)
        kv_state = (k_all, v_all)
        ma = _prompt_attn(qp, kp, vt, gsa)
        yp = _out_proj(yp.reshape(b * s, d), ma.reshape(b * s, ATT_WIDTH),
                       mls.reshape(b * s, LRU_WIDTH + SC_WIDTH), wo, fg,
                       rows=OUT_ROWS, final_norm=last).reshape(b, s, d)
        sp.append((logf, rgh.reshape(b, LRU_WIDTH), rgc, scc))

        qs, ks, vs, gsas, mlss, logfs, rghs, rgcs, sccs = _sample_in(
            ys, lw, state_rg_h[l], jnp.swapaxes(state_rg_conv[l], 0, 1),
            jnp.swapaxes(state_sc_conv[l], 0, 1))
        heads = lambda a: a.reshape(n, N_HEADS, HEAD_DIM)
        mas = _decode_attn(l, page_table, cache_k, cache_v, cache_logf_t, heads(qs), heads(ks),
                           heads(vs), logfs.reshape(n, N_HEADS, 1), heads(gsas))
        ys = _out_proj(ys, mas.reshape(n, ATT_WIDTH), mlss, wo, fg, rows=n, final_norm=last)
        ss.append((ks.reshape(n, 1, N_HEADS, HEAD_DIM), vs.reshape(n, 1, N_HEADS, HEAD_DIM),
                   logfs.reshape(n, 1, N_HEADS), rghs, jnp.swapaxes(rgcs, 0, 1),
                   jnp.swapaxes(sccs, 0, 1)))

    stack = lambda states, i: jnp.stack([st[i] for st in states])
    return (yp, ys.reshape(n, 1, d),
            kv_state[0], kv_state[1], stack(sp, 0), stack(sp, 1), stack(sp, 2), stack(sp, 3),
            stack(ss, 0), stack(ss, 1), stack(ss, 2), stack(ss, 3), stack(ss, 4), stack(ss, 5))
```

```python
import functools

import numpy as np
import jax
import jax.numpy as jnp
from jax import lax
from jax.experimental import pallas as pl
from jax.experimental.pallas import tpu as pltpu

F32 = jnp.float32
BF16 = jnp.bfloat16

N_HEADS = 8
HEAD_DIM = 128
ATT_WIDTH = N_HEADS * HEAD_DIM
LRU_WIDTH = 512
LRU_BLOCKS = 8
LRU_CONV = 4
LRU_C = 8.0
SC_WIDTH = 512
SC_CONV = 3
EPS = 1e-6
_OFF_FL = 3 * ATT_WIDTH
_OFF_GA = _OFF_FL + N_HEADS
_OFF_XR = _OFF_GA + ATT_WIDTH

LANES = 128
SUBLANES = 8
VMEM_LIMIT_BYTES = 60 * 1024 * 1024

PROJ_ROWS = 512
ATTN_BLOCK = 512
OUT_ROWS = 512
DEC_PAGES = 16
PROJ_COLS = 2 * HEAD_DIM

LOG2E = 1.4426950408889634
AUG_DIM = 2 * HEAD_DIM
_SPLIT_LANES = (0, 16, 32)
_ONES_LANE = N_HEADS


def _sigmoid(x):
    return 1.0 / (1.0 + jnp.exp(-x))


def _silu(x):
    return x * _sigmoid(x)


def _log_sigmoid(x):
    return jnp.minimum(x, 0.0) - jnp.log1p(jnp.exp(-jnp.abs(x)))


def _softplus(x):
    return jnp.maximum(x, 0.0) + jnp.log1p(jnp.exp(-jnp.abs(x)))


def _rms_scale(x):
    return lax.rsqrt(jnp.mean(x * x, axis=-1, keepdims=True) + EPS)


def _dot(a, b):
    return jnp.dot(a, b, preferred_element_type=F32)


def _dot_nt(a, b):
    return lax.dot_general(a, b, (((1,), (1,)), ((), ())), preferred_element_type=F32)


def _split3(x):
    hi = x.astype(BF16).astype(F32)
    r = x - hi
    mid = r.astype(BF16).astype(F32)
    lo = (r - mid).astype(BF16).astype(F32)
    return hi, mid, lo


def _lru_gates(xc, wa, ba, wx, bx, lam):
    xb = xc.astype(BF16)
    r = _sigmoid(_dot(xb, wa) + ba)
    i = _sigmoid(_dot(xb, wx) + bx)
    a = jnp.exp((-LRU_C) * r * _softplus(-lam))
    drive = jnp.sqrt(1.0 - a * a) * (i * xc)
    return a, drive


def _bias_feature_matrix():
    m = np.zeros((LANES, 2 * ATT_WIDTH), np.float32)
    for h in range(N_HEADS):
        for part, lane0 in enumerate(_SPLIT_LANES):
            m[lane0 + h, h * HEAD_DIM + part] = 1.0
            m[lane0 + h, ATT_WIDTH + h * HEAD_DIM + 3 + part] = -1.0
            m[_ONES_LANE, h * HEAD_DIM + 3 + part] = 1.0
            m[_ONES_LANE, ATT_WIDTH + h * HEAD_DIM + part] = 1.0
    return jnp.asarray(m, BF16)


def _prompt_in_kernel(x_ref, g_ref, wqkvg_ref, wrest_ref, wfl_ref, bf_ref, sel_ref, cw_ref, cb_ref,
                      wa_ref, ba_ref, wx_ref, bx_ref, lam_ref, scw_ref, *refs, rows, n_state_in):
    (qp_ref, kp_ref, k_ref, v_ref, vt_ref, gsa_ref, mls_ref, logf_ref, rgh_ref, rgc_ref, scc_ref,
     ext_rg, ext_sc, hcar_ref, ccar_ref) = refs[n_state_in:]
    t = pl.program_id(1)
    last_t = pl.num_programs(1) - 1

    @pl.when(t == 0)
    def _():
        ext_rg[0:SUBLANES, :] = jnp.zeros((SUBLANES, LRU_WIDTH), F32)
        ext_sc[0:SUBLANES, :] = jnp.zeros((SUBLANES, SC_WIDTH), F32)
        hcar_ref[...] = jnp.zeros_like(hcar_ref)
        ccar_ref[...] = jnp.zeros_like(ccar_ref)

    x = x_ref[...]
    h = (x * _rms_scale(x) * g_ref[...]).astype(BF16)

    logf = _log_sigmoid(_dot(h, wfl_ref[...]) + bf_ref[...])
    logf_ref[...] = logf[:, 0:N_HEADS]
    row = lax.broadcasted_iota(jnp.int32, (rows, LANES), 0)
    lane = lax.broadcasted_iota(jnp.int32, (rows, LANES), 1)
    cs = logf
    shift = 1
    while shift < rows:
        cs = cs + jnp.where(row >= shift, pltpu.roll(cs, shift, 0), 0.0)
        shift *= 2
    cs = cs + ccar_ref[...]
    ccar_ref[...] = cs[rows - 1:rows, :]
    cx = jnp.where(lane < N_HEADS, cs * LOG2E, jnp.where(lane == _ONES_LANE, 1.0, 0.0))
    hi, mid, lo = _split3(cx)
    packed = (hi + pltpu.roll(mid, _SPLIT_LANES[1], 1)
              + pltpu.roll(lo, _SPLIT_LANES[2], 1)).astype(BF16)

    ch = PROJ_COLS
    heads_per_chunk = ch // HEAD_DIM

    def attn_chunk(c):
        lo_c, hi_c = c * ch, (c + 1) * ch
        q = _dot(h, wqkvg_ref[:, lo_c:hi_c]) * (HEAD_DIM ** -0.5 * LOG2E)
        k = _dot(h, wqkvg_ref[:, ATT_WIDTH + lo_c:ATT_WIDTH + hi_c])
        v = _dot(h, wqkvg_ref[:, 2 * ATT_WIDTH + lo_c:2 * ATT_WIDTH + hi_c])
        ga = _dot(h, wqkvg_ref[:, 3 * ATT_WIDTH + lo_c:3 * ATT_WIDTH + hi_c])
        augq = _dot(packed, sel_ref[:, lo_c:hi_c])
        augk = _dot(packed, sel_ref[:, ATT_WIDTH + lo_c:ATT_WIDTH + hi_c])
        vt_ref[lo_c:hi_c, :] = v.T.astype(BF16)
        gsa_ref[:, lo_c:hi_c] = _silu(ga).astype(BF16)
        for hh in range(heads_per_chunk):
            hd = c * heads_per_chunk + hh
            src = slice(hh * HEAD_DIM, (hh + 1) * HEAD_DIM)
            qp_ref[:, hd * AUG_DIM:hd * AUG_DIM + HEAD_DIM] = q[:, src].astype(BF16)
            qp_ref[:, hd * AUG_DIM + HEAD_DIM:(hd + 1) * AUG_DIM] = augq[:, src].astype(BF16)
            kp_ref[:, hd * AUG_DIM:hd * AUG_DIM + HEAD_DIM] = k[:, src].astype(BF16)
            kp_ref[:, hd * AUG_DIM + HEAD_DIM:(hd + 1) * AUG_DIM] = augk[:, src].astype(BF16)
        return k, v

    rmod =lax.broadcasted_iota(jnp.int32, (rows, ch), 0) % SUBLANES

    def lru_chunk(c):
        cols = slice(c * ch, (c + 1) * ch)
        xr = _dot(h, wrest_ref[:, c * ch:(c + 1) * ch])
        gr = _dot(h, wrest_ref[:, LRU_WIDTH + c * ch:LRU_WIDTH + (c + 1) * ch])
        ext_rg[SUBLANES:SUBLANES + rows, cols] = xr
        xc = cb_ref[:, cols] + cw_ref[LRU_CONV - 1:LRU_CONV, cols] * xr
        for j in range(LRU_CONV - 1):
            o = SUBLANES - (LRU_CONV - 1) + j
            xc = xc + cw_ref[j:j + 1, cols] * ext_rg[o:o + rows, cols]
        tail_c = ext_rg[rows:rows + SUBLANES, cols]
        ext_rg[0:SUBLANES, cols] = tail_c
        a, drive = _lru_gates(xc, wa_ref[cols, cols], ba_ref[:, cols], wx_ref[cols, cols],
                              bx_ref[:, cols], lam_ref[:, cols])
        shift = 1
        while shift < SUBLANES:
            keep = rmod >= shift
            a_sh = pltpu.roll(a, shift, 0)
            d_sh = pltpu.roll(drive, shift, 0)
            drive = jnp.where(keep, a * d_sh + drive, drive)
            a = jnp.where(keep, a * a_sh, a)
            shift *= 2
        hprev = hcar_ref[:, cols]
        groups = []
        for gi in range(rows // SUBLANES):
            sl = slice(gi * SUBLANES, (gi + 1) * SUBLANES)
            hg = a[sl, :] * hprev + drive[sl, :]
            groups.append(hg)
            hprev = hg[SUBLANES - 1:SUBLANES, :]
        hcar_ref[:, cols] = hprev
        lru = jnp.concatenate(groups, axis=0)
        mls_ref[:, cols] = (lru * _silu(gr)).astype(BF16)
        return hprev, tail_c

    def conv_chunk(c):
        cols = slice(c * ch, (c + 1) * ch)
        o0 = 2 * LRU_WIDTH + c * ch
        bc = _dot(h, wrest_ref[:, o0:o0 + ch])
        cc = _dot(h, wrest_ref[:, o0 + SC_WIDTH:o0 + SC_WIDTH + ch])
        hc = _dot(h, wrest_ref[:, o0 + 2 * SC_WIDTH:o0 + 2 * SC_WIDTH + ch])
        gc = _dot(h, wrest_ref[:, o0 + 3 * SC_WIDTH:o0 + 3 * SC_WIDTH + ch])
        u = cc * hc
        ext_sc[SUBLANES:SUBLANES + rows, cols] = u
        cu = scw_ref[SC_CONV - 1:SC_CONV, cols] * u
        for j in range(SC_CONV - 1):
            o = SUBLANES - (SC_CONV - 1) + j
            cu = cu + scw_ref[j:j + 1, cols] * ext_sc[o:o + rows, cols]
        tail_c = ext_sc[rows:rows + SUBLANES, cols]
        ext_sc[0:SUBLANES, cols] = tail_c
        mls_ref[:, LRU_WIDTH + c * ch:LRU_WIDTH + (c + 1) * ch] = (bc * cu * _silu(gc)).astype(BF16)
        return tail_c

    kv = [attn_chunk(c) for c in range(ATT_WIDTH // ch)]
    k_ref[...] = jnp.concatenate([c[0] for c in kv], axis=1).reshape(rows, N_HEADS, HEAD_DIM)
    v_ref[...] = jnp.concatenate([c[1] for c in kv], axis=1).reshape(rows, N_HEADS, HEAD_DIM)
    lru_out = [lru_chunk(c) for c in range(LRU_WIDTH // ch)]
    hlast = jnp.concatenate([o[0] for o in lru_out], axis=1)
    tail = jnp.concatenate([o[1] for o in lru_out], axis=1)
    tail_sc = jnp.concatenate([conv_chunk(c) for c in range(SC_WIDTH // ch)], axis=1)

    @pl.when(t == last_t)
    def _():
        rgh_ref[...] = hlast
        rgc_ref[...] = tail[SUBLANES - (LRU_CONV - 1):SUBLANES, :]
        scc_ref[...] = tail_sc[SUBLANES - (SC_CONV - 1):SUBLANES, :]


def _const_spec(shape):
    nd = len(shape)
    return pl.BlockSpec(shape, lambda *_: (0,) * nd, pipeline_mode=pl.Buffered(1))


def _prompt_in(x, lw, sel, layer, depth, kv_state):
    b, s, d = x.shape
    rows = PROJ_ROWS
    assert rows == ATTN_BLOCK and s % rows == 0
    nt = s // rows
    row_spec = lambda w: pl.BlockSpec((None, rows, w), lambda bi, ti: (bi, ti, 0))
    last_spec = lambda r, w: pl.BlockSpec((None, r, w), lambda bi, ti: (bi, 0, 0))
    state_spec = pl.BlockSpec((None, None, rows, N_HEADS, HEAD_DIM),
                              lambda bi, ti: (layer, bi, ti, 0, 0))
    weights = (lw["g"], lw["w_qkvg"], lw["w_rest"], lw["w_fl"], lw["b_f"], sel, lw["rg_conv_w"],
               lw["rg_conv_b"], lw["wa"], lw["ba"], lw["wx"], lw["bx"], lw["lam"], lw["sc_conv_w"])
    state_in = () if kv_state is None else tuple(kv_state)
    n_in = 1 + len(weights)
    aliases = {n_in + i: 2 + i for i in range(len(state_in))}
    out_shape = (
        jax.ShapeDtypeStruct((b, s, N_HEADS * AUG_DIM), BF16),
        jax.ShapeDtypeStruct((b, s, N_HEADS * AUG_DIM), BF16),
        jax.ShapeDtypeStruct((depth, b, s, N_HEADS, HEAD_DIM), F32),
        jax.ShapeDtypeStruct((depth, b, s, N_HEADS, HEAD_DIM), F32),
        jax.ShapeDtypeStruct((b, nt, ATT_WIDTH, rows), BF16),
        jax.ShapeDtypeStruct((b, s, ATT_WIDTH), BF16),
        jax.ShapeDtypeStruct((b, s, LRU_WIDTH + SC_WIDTH), BF16),
        jax.ShapeDtypeStruct((b, s, N_HEADS), F32),
        jax.ShapeDtypeStruct((b, 1, LRU_WIDTH), F32),
        jax.ShapeDtypeStruct((b, LRU_CONV - 1, LRU_WIDTH), F32),
        jax.ShapeDtypeStruct((b, SC_CONV - 1, SC_WIDTH), F32),
    )
    out_specs = (
        row_spec(N_HEADS * AUG_DIM), row_spec(N_HEADS * AUG_DIM),
        state_spec, state_spec,
        pl.BlockSpec((None, None, ATT_WIDTH, rows), lambda bi, ti: (bi, ti, 0, 0)),
        row_spec(ATT_WIDTH), row_spec(LRU_WIDTH + SC_WIDTH), row_spec(N_HEADS),
        last_spec(1, LRU_WIDTH), last_spec(LRU_CONV - 1, LRU_WIDTH), last_spec(SC_CONV - 1, SC_WIDTH),
    )
    return pl.pallas_call(
        functools.partial(_prompt_in_kernel, rows=rows, n_state_in=len(state_in)),
        out_shape=out_shape,
        grid=(b, nt),
        in_specs=([row_spec(d)] + [_const_spec(w.shape) for w in weights]
                  + [pl.BlockSpec(memory_space=pl.ANY)] * len(state_in)),
        out_specs=out_specs,
        scratch_shapes=[
            pltpu.VMEM((rows + SUBLANES, LRU_WIDTH), F32),
            pltpu.VMEM((rows + SUBLANES, SC_WIDTH), F32),
            pltpu.VMEM((1, LRU_WIDTH), F32),
            pltpu.VMEM((1, LANES), F32),
        ],
        input_output_aliases=aliases,
        compiler_params=pltpu.CompilerParams(
            dimension_semantics=("arbitrary", "arbitrary"), vmem_limit_bytes=VMEM_LIMIT_BYTES),
        name="prompt_in",
    )(x, *weights, *state_in)


def _attn_kv_block(qp_ref, kp_ref, vt_ref, m_ref, l_ref, acc_ref, j, masked):
    blk = ATTN_BLOCK
    r0 = pl.multiple_of(j * blk, blk)
    for hd in range(N_HEADS):
        kblk = kp_ref[pl.ds(r0, blk), hd * AUG_DIM:(hd + 1) * AUG_DIM]
        s = _dot_nt(kblk, qp_ref[:, hd * AUG_DIM:(hd + 1) * AUG_DIM])
        if masked:
            krow = lax.broadcasted_iota(jnp.int32, (blk, blk), 0)
            qcol = lax.broadcasted_iota(jnp.int32, (blk, blk), 1)
            s = jnp.where(krow <= qcol, s, -jnp.inf)
        m_old = m_ref[hd]
        m_new = jnp.maximum(m_old, jnp.max(s, axis=0, keepdims=True))
        alpha = jnp.exp2(m_old - m_new)
        p = jnp.exp2(s - m_new)
        l_ref[hd] = alpha * l_ref[hd] + jnp.sum(p, axis=0, keepdims=True)
        m_ref[hd] = m_new
        vt = vt_ref[j, hd * HEAD_DIM:(hd + 1) * HEAD_DIM, :]
        acc_ref[hd] = alpha * acc_ref[hd] + _dot(vt, p.astype(BF16))


def _out_kernel(x_ref, ma_ref, mls_ref, wo_ref, fg_ref, y_ref, *, final_norm):
    y = x_ref[...] + _dot(ma_ref[...].astype(BF16), wo_ref[0:ATT_WIDTH, :])
    y = y + _dot(mls_ref[...], wo_ref[ATT_WIDTH:, :])
    if final_norm:
        y = y * _rms_scale(y) * fg_ref[...]
    y_ref[...] = y


def _out_proj(x2, ma2, mls2, wo, fg, *, rows, final_norm):
    n, d = x2.shape
    spec = lambda w: pl.BlockSpec((rows, w), lambda i: (i, 0))
    return pl.pallas_call(
        functools.partial(_out_kernel, final_norm=final_norm),
        out_shape=jax.ShapeDtypeStruct((n, d), F32),
        grid=(n // rows,),
        in_specs=[spec(d), spec(ATT_WIDTH), spec(LRU_WIDTH + SC_WIDTH),
                  _const_spec(wo.shape), _const_spec(fg.shape)],
        out_specs=spec(d),
        compiler_params=pltpu.CompilerParams(
            dimension_semantics=("arbitrary",), vmem_limit_bytes=VMEM_LIMIT_BYTES),
        name="out_proj",
    )(x2, ma2, mls2, wo, fg)


def _sample_in_kernel(x_ref, g_ref, wqkvg_ref, wrest_ref, wfl_ref, bf_ref, cw_ref, cb_ref,
                      wa_ref, ba_ref, wx_ref, bx_ref, lam_ref, scw_ref, h0_ref, rgbuf_ref, scbuf_ref,
                      q_ref, k_ref, v_ref, gsa_ref, mls_ref, logf_ref, rgh_ref, rgc_ref, scc_ref):
    x = x_ref[...]
    h = (x * _rms_scale(x) * g_ref[...]).astype(BF16)
    q_ref[...] = _dot(h, wqkvg_ref[:, 0:ATT_WIDTH]) * (HEAD_DIM ** -0.5)
    k_ref[...] = _dot(h, wqkvg_ref[:, ATT_WIDTH:2 * ATT_WIDTH])
    v_ref[...] = _dot(h, wqkvg_ref[:, 2 * ATT_WIDTH:3 * ATT_WIDTH])
    gsa_ref[...] = _silu(_dot(h, wqkvg_ref[:, 3 * ATT_WIDTH:4 * ATT_WIDTH]))
    logf = _log_sigmoid(_dot(h, wfl_ref[...]) + bf_ref[...])
    logf_ref[...] = logf[:, 0:N_HEADS]

    xr = _dot(h, wrest_ref[:, 0:LRU_WIDTH])
    gr = _dot(h, wrest_ref[:, LRU_WIDTH:2 * LRU_WIDTH])
    xc = cb_ref[...] + cw_ref[LRU_CONV - 1:LRU_CONV, :] * xr
    for j in range(LRU_CONV - 1):
        xc = xc + cw_ref[j:j + 1, :] * rgbuf_ref[j]
    for j in range(LRU_CONV - 2):
        rgc_ref[j] = rgbuf_ref[j + 1]
    rgc_ref[LRU_CONV - 2] = xr
    a, drive = _lru_gates(xc, wa_ref[...], ba_ref[...], wx_ref[...], bx_ref[...], lam_ref[...])
    hn = a * h0_ref[...] + drive
    rgh_ref[...] = hn
    mls_ref[:, 0:LRU_WIDTH] = (hn * _silu(gr)).astype(BF16)

    o0 = 2 * LRU_WIDTH
    bc = _dot(h, wrest_ref[:, o0:o0 + SC_WIDTH])
    cc = _dot(h, wrest_ref[:, o0 + SC_WIDTH:o0 + 2 * SC_WIDTH])
    hc = _dot(h, wrest_ref[:, o0 + 2 * SC_WIDTH:o0 + 3 * SC_WIDTH])
    gc = _dot(h, wrest_ref[:, o0 + 3 * SC_WIDTH:o0 + 4 * SC_WIDTH])
    u = cc * hc
    cu = scw_ref[SC_CONV - 1:SC_CONV, :] * u
    for j in range(SC_CONV - 1):
        cu = cu + scw_ref[j:j + 1, :] * scbuf_ref[j]
    for j in range(SC_CONV - 2):
        scc_ref[j] = scbuf_ref[j + 1]
    scc_ref[SC_CONV - 2] = u
    mls_ref[:, LRU_WIDTH:LRU_WIDTH + SC_WIDTH] = (bc * cu * _silu(gc)).astype(BF16)


def _sample_in(x2, lw, h0, rgbuf_t, scbuf_t):
    n, _ = x2.shape
    weights = (lw["g"], lw["w_qkvg"], lw["w_rest"], lw["w_fl"], lw["b_f"], lw["rg_conv_w"],
               lw["rg_conv_b"], lw["wa"], lw["ba"], lw["wx"], lw["bx"], lw["lam"], lw["sc_conv_w"])
    out_shape = (
        jax.ShapeDtypeStruct((n, ATT_WIDTH), F32),
        jax.ShapeDtypeStruct((n, ATT_WIDTH), F32),
        jax.ShapeDtypeStruct((n, ATT_WIDTH), F32),
        jax.ShapeDtypeStruct((n, ATT_WIDTH), F32),
        jax.ShapeDtypeStruct((n, LRU_WIDTH + SC_WIDTH), BF16),
        jax.ShapeDtypeStruct((n, N_HEADS), F32),
        jax.ShapeDtypeStruct((n, LRU_WIDTH), F32),
        jax.ShapeDtypeStruct((LRU_CONV - 1, n, LRU_WIDTH), F32),
        jax.ShapeDtypeStruct((SC_CONV - 1, n, SC_WIDTH), F32),
    )
    operands = (x2,) + weights + (h0, rgbuf_t, scbuf_t)
    full = lambda shape: pl.BlockSpec(shape, lambda i, nd=len(shape): (0,) * nd)
    return pl.pallas_call(
        _sample_in_kernel,
        out_shape=out_shape,
        grid=(1,),
        in_specs=[_const_spec(o.shape) for o in operands],
        out_specs=tuple(full(o.shape) for o in out_shape),
        compiler_params=pltpu.CompilerParams(
            dimension_semantics=("arbitrary",), vmem_limit_bytes=VMEM_LIMIT_BYTES),
        name="sample_in",
    )(*operands)


def _decode_step(j, n_steps, q_ref, kn_ref, vn_ref, fn_ref, gsa_ref, page_refs, o_ref, scratch):
    npg = DEC_PAGES
    k_refs = page_refs[0:npg]
    v_refs = page_refs[npg:2 * npg]
    f_refs = page_refs[2 * npg:3 * npg]
    qb_ref, m_ref, l_ref, acc_ref, tot_ref = scratch
    page = LANES
    width = page * N_HEADS

    @pl.when(j == 0)
    def _():
        q = q_ref[...]
        qb_ref[...] = q.astype(BF16)
        m_ref[...] = jnp.sum(q * kn_ref[...], axis=1, keepdims=True)
        l_ref[...] = jnp.ones_like(l_ref)
        acc_ref[...] = vn_ref[...]
        tot_ref[...] = jnp.broadcast_to(fn_ref[...], (N_HEADS, LANES))

    ri = lax.broadcasted_iota(jnp.int32, (page, 2 * page), 0)
    ci = lax.broadcasted_iota(jnp.int32, (page, 2 * page), 1)
    u2 = jnp.where((ri > ci) | (ci >= page), 1.0, 0.0).astype(F32)
    rs = lax.broadcasted_iota(jnp.int32, (page, width), 0)
    cs = lax.broadcasted_iota(jnp.int32, (page, width), 1)
    spread = jnp.where(cs // N_HEADS == rs, 1.0, 0.0).astype(BF16)
    own = (lax.broadcasted_iota(jnp.int32, (N_HEADS, width), 1) % N_HEADS
           == lax.broadcasted_iota(jnp.int32, (N_HEADS, width), 0))

    tot = tot_ref[...]
    bias = []
    for i in range(npg):
        suf = jnp.dot(f_refs[i][...], u2, precision=lax.Precision.HIGHEST,
                      preferred_element_type=F32)
        bias.append(tot + suf[:, 0:page])
        tot = tot + suf[:, page:2 * page]
    tot_ref[...] = tot
    hi, mid, lo = _split3(jnp.concatenate(bias, axis=0))
    parts = jnp.concatenate([hi, mid, lo], axis=0).astype(BF16)
    spread_parts = _dot(parts, spread)
    rows = npg * N_HEADS
    bias_w = spread_parts[0:rows] + spread_parts[rows:2 * rows] + spread_parts[2 * rows:3 * rows]

    qb = qb_ref[...]
    scores = []
    for i in range(npg):
        kpage = k_refs[i][...].reshape(width, HEAD_DIM).astype(BF16)
        s = _dot_nt(qb, kpage) + bias_w[i * N_HEADS:(i + 1) * N_HEADS]
        scores.append(jnp.where(own, s, -jnp.inf))
    s_all = jnp.concatenate(scores, axis=1)
    m_old = m_ref[...]
    m_new = jnp.maximum(m_old, jnp.max(s_all, axis=1, keepdims=True))
    alpha = jnp.exp(m_old - m_new)
    p = jnp.exp(s_all - m_new)
    l_ref[...] = alpha * l_ref[...] + jnp.sum(p, axis=1, keepdims=True)
    m_ref[...] = m_new
    pb = p.astype(BF16)
    acc = alpha * acc_ref[...]
    for i in range(npg):
        vpage = v_refs[i][...].reshape(width, HEAD_DIM).astype(BF16)
        acc = acc + _dot(pb[:, i * width:(i + 1) * width], vpage)
    acc_ref[...] = acc

    @pl.when(j == n_steps - 1)
    def _():
        o_ref[...] = acc / l_ref[...] * gsa_ref[...]


def _attn_decode_kernel(pt_ref, qp_ref, kp_ref, vt_ref, gsa_ref, q_ref, kn_ref, vn_ref, fn_ref,
                        gsd_ref, *refs):
    npg = DEC_PAGES
    page_refs = refs[0:3 * npg]
    o_ref, od_ref = refs[3 * npg:3 * npg + 2]
    m_ref, l_ref, acc_ref = refs[3 * npg + 2:3 * npg + 5]
    dec_scratch = refs[3 * npg + 5:]
    qi = pl.program_id(1)
    c = pl.program_id(2)

    @pl.when(c == 0)
    def _():
        m_ref[...] = jnp.full(m_ref.shape, -jnp.inf, F32)
        l_ref[...] = jnp.zeros(l_ref.shape, F32)
        acc_ref[...] = jnp.zeros(acc_ref.shape, F32)

    @pl.when(c < qi)
    def _():
        _attn_kv_block(qp_ref, kp_ref, vt_ref, m_ref, l_ref, acc_ref, c, False)

    @pl.when(c == qi)
    def _():
        _attn_kv_block(qp_ref, kp_ref, vt_ref, m_ref, l_ref, acc_ref, c, True)
        for hd in range(N_HEADS):
            lo, hi = hd * HEAD_DIM, (hd + 1) * HEAD_DIM
            att = (acc_ref[hd] / l_ref[hd]).T
            o_ref[:, lo:hi] = (att * gsa_ref[:, lo:hi].astype(F32)).astype(BF16)

    _decode_step(c, pl.num_programs(2), q_ref, kn_ref, vn_ref, fn_ref, gsd_ref,
                 page_refs, od_ref, dec_scratch)


def _attn_decode(layer, page_table, qp, kp, vt, gsa, cache_k, cache_v, cache_logf_t,
                 q3, kn3, vn3, fn3, gsa3):
    b, s, _ = qp.shape
    blk = ATTN_BLOCK
    nq = s // blk
    n, n_pages = page_table.shape
    page = cache_k.shape[2]
    steps = n_pages // DEC_PAGES
    assert page == LANES and n_pages % DEC_PAGES == 0 and n == b * nq and steps >= nq
    qspec = pl.BlockSpec((None, blk, N_HEADS * AUG_DIM), lambda bi, qi, c, pt: (bi, qi, 0))
    ospec = pl.BlockSpec((None, blk, ATT_WIDTH), lambda bi, qi, c, pt: (bi, qi, 0))
    tok = pl.BlockSpec((None, N_HEADS, HEAD_DIM), lambda bi, qi, c, pt: (bi * nq + qi, 0, 0))

    def page_spec(i, tail):
        def imap(bi, qi, c, pt):
            return (layer, pt[bi * nq + qi, n_pages - 1 - (c * DEC_PAGES + i)]) + (0,) * len(tail)
        return pl.BlockSpec((None, None) + tail, imap)

    kv_tail = (page, N_HEADS, HEAD_DIM)
    per_row = dict(pipeline_mode=pl.Buffered(1))
    in_specs = ([qspec,
                 pl.BlockSpec((None, s, N_HEADS * AUG_DIM), lambda bi, qi, c, pt: (bi, 0, 0),
                              **per_row),
                 pl.BlockSpec((None, nq, ATT_WIDTH, blk), lambda bi, qi, c, pt: (bi, 0, 0, 0),
                              **per_row),
                 ospec,
                 tok, tok, tok,
                 pl.BlockSpec((None, N_HEADS, 1), lambda bi, qi, c, pt: (bi * nq + qi, 0, 0)), tok]
                + [page_spec(i, kv_tail) for i in range(DEC_PAGES)]
                + [page_spec(i, kv_tail) for i in range(DEC_PAGES)]
                + [page_spec(i, (N_HEADS, page)) for i in range(DEC_PAGES)])
    grid_spec = pltpu.PrefetchScalarGridSpec(
        num_scalar_prefetch=1,
        grid=(b, nq, steps),
        in_specs=in_specs,
        out_specs=(ospec, tok),
        scratch_shapes=[
            pltpu.VMEM((N_HEADS, 1, blk), F32),
            pltpu.VMEM((N_HEADS, 1, blk), F32),
            pltpu.VMEM((N_HEADS, HEAD_DIM, blk), F32),
            pltpu.VMEM((N_HEADS, HEAD_DIM), BF16),
            pltpu.VMEM((N_HEADS, 1), F32),
            pltpu.VMEM((N_HEADS, 1), F32),
            pltpu.VMEM((N_HEADS, HEAD_DIM), F32),
            pltpu.VMEM((N_HEADS, LANES), F32),
        ],
    )
    return pl.pallas_call(
        _attn_decode_kernel,
        out_shape=(jax.ShapeDtypeStruct((b, s, ATT_WIDTH), BF16),
                   jax.ShapeDtypeStruct((n, N_HEADS, HEAD_DIM), F32)),
        grid_spec=grid_spec,
        compiler_params=pltpu.CompilerParams(
            dimension_semantics=("arbitrary", "arbitrary", "arbitrary"),
            vmem_limit_bytes=VMEM_LIMIT_BYTES),
        name="attn_decode",
    )(page_table, qp, kp, vt, gsa, q3, kn3, vn3, fn3, gsa3,
      *([cache_k] * DEC_PAGES), *([cache_v] * DEC_PAGES), *([cache_logf_t] * DEC_PAGES))


def _block_diag(w):
    g, i, j = w.shape
    eye = jnp.eye(g, dtype=w.dtype)
    return (eye[:, None, :, None] * w[:, :, None, :]).reshape(g * i, g * j)


def _layer_weights(l, norm_g, w_in, b_f, rg_conv_w, rg_conv_b, rg_wa, rg_ba, rg_wx, rg_bx,
                   rg_lambda, sc_conv_w):
    w = w_in[l]
    row = lambda a: a.reshape(1, -1).astype(F32)
    return dict(
        g=row(norm_g[l]),
        w_qkvg=jnp.concatenate([w[:, 0:_OFF_FL], w[:, _OFF_GA:_OFF_XR]], axis=1).astype(BF16),
        w_rest=w[:, _OFF_XR:].astype(BF16),
        w_fl=jnp.pad(w[:, _OFF_FL:_OFF_GA], ((0, 0), (0, LANES - N_HEADS))).astype(BF16),
        b_f=jnp.pad(row(b_f[l]), ((0, 0), (0, LANES - N_HEADS))),
        rg_conv_w=rg_conv_w[l].astype(F32), rg_conv_b=row(rg_conv_b[l]),
        wa=_block_diag(rg_wa[l]).astype(BF16), ba=row(rg_ba[l]),
        wx=_block_diag(rg_wx[l]).astype(BF16), bx=row(rg_bx[l]),
        lam=row(rg_lambda[l]), sc_conv_w=sc_conv_w[l].astype(F32),
    )


def kernel(x_prompt, x_sample, cache_k, cache_v, cache_logf, state_rg_h, state_rg_conv, state_sc_conv, page_table, norm_g, w_in, b_f, rg_conv_w, rg_conv_b, rg_wa, rg_ba, rg_wx, rg_bx, rg_lambda, sc_conv_w, w_out, final_g):
    depth = w_in.shape[0]
    b, s, d = x_prompt.shape
    n = x_sample.shape[0]
    cache_logf_t = jnp.swapaxes(cache_logf, 2, 3)
    fg = final_g.reshape(1, d).astype(F32)
    sel = _bias_feature_matrix()

    yp = x_prompt
    ys = x_sample.reshape(n, d)
    sp, ss = [], []
    kv_state = None
    for l in range(depth):
        lw = _layer_weights(l, norm_g, w_in, b_f, rg_conv_w, rg_conv_b, rg_wa, rg_ba, rg_wx,
                            rg_bx, rg_lambda, sc_conv_w)
        wo = w_out[l].astype(BF16)
        last = l == depth - 1

        qp, kp, k_all, v_all, vt, gsa, mls, logf, rgh, rgc, scc = _prompt_in(
            yp, lw, sel, l, depth, kv_state)
        kv_state = (k_all, v_all)
        qs, ks, vs, gsas, mlss, logfs, rghs, rgcs, sccs = _sample_in(
            ys, lw, state_rg_h[l], jnp.swapaxes(state_rg_conv[l], 0, 1),
            jnp.swapaxes(state_sc_conv[l], 0, 1))
        heads = lambda a: a.reshape(n, N_HEADS, HEAD_DIM)
        ma, mas = _attn_decode(l, page_table, qp, kp, vt, gsa, cache_k, cache_v, cache_logf_t,
                               heads(qs), heads(ks), heads(vs), logfs.reshape(n, N_HEADS, 1),
                               heads(gsas))
        yp = _out_proj(yp.reshape(b * s, d), ma.reshape(b * s, ATT_WIDTH),
                       mls.reshape(b * s, LRU_WIDTH + SC_WIDTH), wo, fg,
                       rows=OUT_ROWS, final_norm=last).reshape(b, s, d)
        sp.append((logf, rgh.reshape(b, LRU_WIDTH), rgc, scc))
        ys = _out_proj(ys, mas.reshape(n, ATT_WIDTH), mlss, wo, fg, rows=n, final_norm=last)
        ss.append((ks.reshape(n, 1, N_HEADS, HEAD_DIM), vs.reshape(n, 1, N_HEADS, HEAD_DIM),
                   logfs.reshape(n, 1, N_HEADS), rghs, jnp.swapaxes(rgcs, 0, 1),
                   jnp.swapaxes(sccs, 0, 1)))

    stack = lambda states, i: jnp.stack([st[i] for st in states])
    return (yp, ys.reshape(n, 1, d),
            kv_state[0], kv_state[1], stack(sp, 0), stack(sp, 1), stack(sp, 2), stack(sp, 3),
            stack(ss, 0), stack(ss, 1), stack(ss, 2), stack(ss, 3), stack(ss, 4), stack(ss, 5))
```

```python
import functools

import numpy as np
import jax
import jax.numpy as jnp
from jax import lax
from jax.experimental import pallas as pl
from jax.experimental.pallas import tpu as pltpu

F32 = jnp.float32
BF16 = jnp.bfloat16

N_HEADS = 8
HEAD_DIM = 128
ATT_WIDTH = N_HEADS * HEAD_DIM
LRU_WIDTH = 512
LRU_BLOCKS = 8
LRU_CONV = 4
LRU_C = 8.0
SC_WIDTH = 512
SC_CONV = 3
EPS = 1e-6
_OFF_FL = 3 * ATT_WIDTH
_OFF_GA = _OFF_FL + N_HEADS
_OFF_XR = _OFF_GA + ATT_WIDTH

LANES = 128
SUBLANES = 8
VMEM_LIMIT_BYTES = 60 * 1024 * 1024

PROJ_ROWS = 512
ATTN_BLOCK = 512
OUT_ROWS = 512
DEC_PAGES = 16
PROJ_COLS = 2 * HEAD_DIM

LOG2E = 1.4426950408889634
AUG_DIM = 2 * HEAD_DIM
_SPLIT_LANES = (0, 16, 32)
_ONES_LANE = N_HEADS


def _sigmoid(x):
    return 1.0 / (1.0 + jnp.exp(-x))


def _silu(x):
    return x * _sigmoid(x)


def _log_sigmoid(x):
    return jnp.minimum(x, 0.0) - jnp.log1p(jnp.exp(-jnp.abs(x)))


def _softplus(x):
    return jnp.maximum(x, 0.0) + jnp.log1p(jnp.exp(-jnp.abs(x)))


def _rms_scale(x):
    return lax.rsqrt(jnp.mean(x * x, axis=-1, keepdims=True) + EPS)


def _dot(a, b):
    return jnp.dot(a, b, preferred_element_type=F32)


def _dot_nt(a, b):
    return lax.dot_general(a, b, (((1,), (1,)), ((), ())), preferred_element_type=F32)


def _split3(x):
    hi = x.astype(BF16).astype(F32)
    r = x - hi
    mid = r.astype(BF16).astype(F32)
    lo = (r - mid).astype(BF16).astype(F32)
    return hi, mid, lo


def _lru_gates(xc, wa, ba, wx, bx, lam):
    xb = xc.astype(BF16)
    r = _sigmoid(_dot(xb, wa) + ba)
    i = _sigmoid(_dot(xb, wx) + bx)
    a = jnp.exp((-LRU_C) * r * _softplus(-lam))
    drive = jnp.sqrt(1.0 - a * a) * (i * xc)
    return a, drive


def _bias_feature_matrix():
    m = np.zeros((LANES, 2 * ATT_WIDTH), np.float32)
    for h in range(N_HEADS):
        for part, lane0 in enumerate(_SPLIT_LANES):
            m[lane0 + h, h * HEAD_DIM + part] = 1.0
            m[lane0 + h, ATT_WIDTH + h * HEAD_DIM + 3 + part] = -1.0
            m[_ONES_LANE, h * HEAD_DIM + 3 + part] = 1.0
            m[_ONES_LANE, ATT_WIDTH + h * HEAD_DIM + part] = 1.0
    return jnp.asarray(m, BF16)


def _prompt_in_kernel(x_ref, g_ref, wqkvg_ref, wrest_ref, wfl_ref, bf_ref, sel_ref, cw_ref, cb_ref,
                      wa_ref, ba_ref, wx_ref, bx_ref, lam_ref, scw_ref, *refs, rows, n_state_in):
    (qp_ref, kp_ref, k_ref, v_ref, vt_ref, gsa_ref, mls_ref, logf_ref, rgh_ref, rgc_ref, scc_ref,
     ext_rg, ext_sc, hcar_ref, ccar_ref) = refs[n_state_in:]
    t = pl.program_id(1)
    last_t = pl.num_programs(1) - 1

    @pl.when(t == 0)
    def _():
        ext_rg[0:SUBLANES, :] = jnp.zeros((SUBLANES, LRU_WIDTH), F32)
        ext_sc[0:SUBLANES, :] = jnp.zeros((SUBLANES, SC_WIDTH), F32)
        hcar_ref[...] = jnp.zeros_like(hcar_ref)
        ccar_ref[...] = jnp.zeros_like(ccar_ref)

    x = x_ref[...]
    h = (x * _rms_scale(x) * g_ref[...]).astype(BF16)

    logf = _log_sigmoid(_dot(h, wfl_ref[...]) + bf_ref[...])
    logf_ref[...] = logf[:, 0:N_HEADS]
    row = lax.broadcasted_iota(jnp.int32, (rows, LANES), 0)
    lane = lax.broadcasted_iota(jnp.int32, (rows, LANES), 1)
    cs = logf
    shift = 1
    while shift < rows:
        cs = cs + jnp.where(row >= shift, pltpu.roll(cs, shift, 0), 0.0)
        shift *= 2
    cs = cs + ccar_ref[...]
    ccar_ref[...] = cs[rows - 1:rows, :]
    cx = jnp.where(lane < N_HEADS, cs * LOG2E, jnp.where(lane == _ONES_LANE, 1.0, 0.0))
    hi, mid, lo = _split3(cx)
    packed = (hi + pltpu.roll(mid, _SPLIT_LANES[1], 1)
              + pltpu.roll(lo, _SPLIT_LANES[2], 1)).astype(BF16)

    ch = PROJ_COLS
    heads_per_chunk = ch // HEAD_DIM

    def attn_chunk(c):
        lo_c, hi_c = c * ch, (c + 1) * ch
        q = _dot(h, wqkvg_ref[:, lo_c:hi_c]) * (HEAD_DIM ** -0.5 * LOG2E)
        k = _dot(h, wqkvg_ref[:, ATT_WIDTH + lo_c:ATT_WIDTH + hi_c])
        v = _dot(h, wqkvg_ref[:, 2 * ATT_WIDTH + lo_c:2 * ATT_WIDTH + hi_c])
        ga = _dot(h, wqkvg_ref[:, 3 * ATT_WIDTH + lo_c:3 * ATT_WIDTH + hi_c])
        augq = _dot(packed, sel_ref[:, lo_c:hi_c])
        augk = _dot(packed, sel_ref[:, ATT_WIDTH + lo_c:ATT_WIDTH + hi_c])
        vt_ref[lo_c:hi_c, :] = v.T.astype(BF16)
        gsa_ref[:, lo_c:hi_c] = _silu(ga).astype(BF16)
        for hh in range(heads_per_chunk):
            hd = c * heads_per_chunk + hh
            src = slice(hh * HEAD_DIM, (hh + 1) * HEAD_DIM)
            qp_ref[:, hd * AUG_DIM:hd * AUG_DIM + HEAD_DIM] = q[:, src].astype(BF16)
            qp_ref[:, hd * AUG_DIM + HEAD_DIM:(hd + 1) * AUG_DIM] = augq[:, src].astype(BF16)
            kp_ref[:, hd * AUG_DIM:hd * AUG_DIM + HEAD_DIM] = k[:, src].astype(BF16)
            kp_ref[:, hd * AUG_DIM + HEAD_DIM:(hd + 1) * AUG_DIM] = augk[:, src].astype(BF16)
        return k, v

    rmod =lax.broadcasted_iota(jnp.int32, (rows, ch), 0) % SUBLANES

    def lru_chunk(c):
        cols = slice(c * ch, (c + 1) * ch)
        xr = _dot(h, wrest_ref[:, c * ch:(c + 1) * ch])
        gr = _dot(h, wrest_ref[:, LRU_WIDTH + c * ch:LRU_WIDTH + (c + 1) * ch])
        ext_rg[SUBLANES:SUBLANES + rows, cols] = xr
        xc = cb_ref[:, cols] + cw_ref[LRU_CONV - 1:LRU_CONV, cols] * xr
        for j in range(LRU_CONV - 1):
            o = SUBLANES - (LRU_CONV - 1) + j
            xc = xc + cw_ref[j:j + 1, cols] * ext_rg[o:o + rows, cols]
        tail_c = ext_rg[rows:rows + SUBLANES, cols]
        ext_rg[0:SUBLANES, cols] = tail_c
        a, drive = _lru_gates(xc, wa_ref[cols, cols], ba_ref[:, cols], wx_ref[cols, cols],
                              bx_ref[:, cols], lam_ref[:, cols])
        shift = 1
        while shift < SUBLANES:
            keep = rmod >= shift
            a_sh = pltpu.roll(a, shift, 0)
            d_sh = pltpu.roll(drive, shift, 0)
            drive = jnp.where(keep, a * d_sh + drive, drive)
            a = jnp.where(keep, a * a_sh, a)
            shift *= 2
        hprev = hcar_ref[:, cols]
        groups = []
        for gi in range(rows // SUBLANES):
            sl = slice(gi * SUBLANES, (gi + 1) * SUBLANES)
            hg = a[sl, :] * hprev + drive[sl, :]
            groups.append(hg)
            hprev = hg[SUBLANES - 1:SUBLANES, :]
        hcar_ref[:, cols] = hprev
        lru = jnp.concatenate(groups, axis=0)
        mls_ref[:, cols] = (lru * _silu(gr)).astype(BF16)
        return hprev, tail_c

    def conv_chunk(c):
        cols = slice(c * ch, (c + 1) * ch)
        o0 = 2 * LRU_WIDTH + c * ch
        bc = _dot(h, wrest_ref[:, o0:o0 + ch])
        cc = _dot(h, wrest_ref[:, o0 + SC_WIDTH:o0 + SC_WIDTH + ch])
        hc = _dot(h, wrest_ref[:, o0 + 2 * SC_WIDTH:o0 + 2 * SC_WIDTH + ch])
        gc = _dot(h, wrest_ref[:, o0 + 3 * SC_WIDTH:o0 + 3 * SC_WIDTH + ch])
        u = cc * hc
        ext_sc[SUBLANES:SUBLANES + rows, cols] = u
        cu = scw_ref[SC_CONV - 1:SC_CONV, cols] * u
        for j in range(SC_CONV - 1):
            o = SUBLANES - (SC_CONV - 1) + j
            cu = cu + scw_ref[j:j + 1, cols] * ext_sc[o:o + rows, cols]
        tail_c = ext_sc[rows:rows + SUBLANES, cols]
        ext_sc[0:SUBLANES, cols] = tail_c
        mls_ref[:, LRU_WIDTH + c * ch:LRU_WIDTH + (c + 1) * ch] = (bc * cu * _silu(gc)).astype(BF16)
        return tail_c

    kv = [attn_chunk(c) for c in range(ATT_WIDTH // ch)]
    k_ref[...] = jnp.concatenate([c[0] for c in kv], axis=1).reshape(rows, N_HEADS, HEAD_DIM)
    v_ref[...] = jnp.concatenate([c[1] for c in kv], axis=1).reshape(rows, N_HEADS, HEAD_DIM)
    lru_out = [lru_chunk(c) for c in range(LRU_WIDTH // ch)]
    hlast = jnp.concatenate([o[0] for o in lru_out], axis=1)
    tail = jnp.concatenate([o[1] for o in lru_out], axis=1)
    tail_sc = jnp.concatenate([conv_chunk(c) for c in range(SC_WIDTH // ch)], axis=1)

    @pl.when(t == last_t)
    def _():
        rgh_ref[...] = hlast
        rgc_ref[...] = tail[SUBLANES - (LRU_CONV - 1):SUBLANES, :]
        scc_ref[...] = tail_sc[SUBLANES - (SC_CONV - 1):SUBLANES, :]


def _const_spec(shape):
    nd = len(shape)
    return pl.BlockSpec(shape, lambda *_: (0,) * nd, pipeline_mode=pl.Buffered(1))


def _prompt_in(x, lw, sel, layer, depth, kv_state):
    b, s, d = x.shape
    rows = PROJ_ROWS
    assert rows == ATTN_BLOCK and s % rows == 0
    nt = s // rows
    row_spec = lambda w: pl.BlockSpec((None, rows, w), lambda bi, ti: (bi, ti, 0))
    last_spec = lambda r, w: pl.BlockSpec((None, r, w), lambda bi, ti: (bi, 0, 0))
    state_spec = pl.BlockSpec((None, None, rows, N_HEADS, HEAD_DIM),
                              lambda bi, ti: (layer, bi, ti, 0, 0))
    weights = (lw["g"], lw["w_qkvg"], lw["w_rest"], lw["w_fl"], lw["b_f"], sel, lw["rg_conv_w"],
               lw["rg_conv_b"], lw["wa"], lw["ba"], lw["wx"], lw["bx"], lw["lam"], lw["sc_conv_w"])
    state_in = () if kv_state is None else tuple(kv_state)
    n_in = 1 + len(weights)
    aliases = {n_in + i: 2 + i for i in range(len(state_in))}
    out_shape = (
        jax.ShapeDtypeStruct((b, s, N_HEADS * AUG_DIM), BF16),
        jax.ShapeDtypeStruct((b, s, N_HEADS * AUG_DIM), BF16),
        jax.ShapeDtypeStruct((depth, b, s, N_HEADS, HEAD_DIM), F32),
        jax.ShapeDtypeStruct((depth, b, s, N_HEADS, HEAD_DIM), F32),
        jax.ShapeDtypeStruct((b, nt, ATT_WIDTH, rows), BF16),
        jax.ShapeDtypeStruct((b, s, ATT_WIDTH), BF16),
        jax.ShapeDtypeStruct((b, s, LRU_WIDTH + SC_WIDTH), BF16),
        jax.ShapeDtypeStruct((b, s, N_HEADS), F32),
        jax.ShapeDtypeStruct((b, 1, LRU_WIDTH), F32),
        jax.ShapeDtypeStruct((b, LRU_CONV - 1, LRU_WIDTH), F32),
        jax.ShapeDtypeStruct((b, SC_CONV - 1, SC_WIDTH), F32),
    )
    out_specs = (
        row_spec(N_HEADS * AUG_DIM), row_spec(N_HEADS * AUG_DIM),
        state_spec, state_spec,
        pl.BlockSpec((None, None, ATT_WIDTH, rows), lambda bi, ti: (bi, ti, 0, 0)),
        row_spec(ATT_WIDTH), row_spec(LRU_WIDTH + SC_WIDTH), row_spec(N_HEADS),
        last_spec(1, LRU_WIDTH), last_spec(LRU_CONV - 1, LRU_WIDTH), last_spec(SC_CONV - 1, SC_WIDTH),
    )
    return pl.pallas_call(
        functools.partial(_prompt_in_kernel, rows=rows, n_state_in=len(state_in)),
        out_shape=out_shape,
        grid=(b, nt),
        in_specs=([row_spec(d)] + [_const_spec(w.shape) for w in weights]
                  + [pl.BlockSpec(memory_space=pl.ANY)] * len(state_in)),
        out_specs=out_specs,
        scratch_shapes=[
            pltpu.VMEM((rows + SUBLANES, LRU_WIDTH), F32),
            pltpu.VMEM((rows + SUBLANES, SC_WIDTH), F32),
            pltpu.VMEM((1, LRU_WIDTH), F32),
            pltpu.VMEM((1, LANES), F32),
        ],
        input_output_aliases=aliases,
        compiler_params=pltpu.CompilerParams(
            dimension_semantics=("arbitrary", "arbitrary"), vmem_limit_bytes=VMEM_LIMIT_BYTES),
        name="prompt_in",
    )(x, *weights, *state_in)


def _attn_kv_block(qp_ref, kp_ref, vt_ref, m_ref, l_ref, acc_ref, j, masked):
    blk = ATTN_BLOCK
    r0 = pl.multiple_of(j * blk, blk)
    for hd in range(N_HEADS):
        kblk = kp_ref[pl.ds(r0, blk), hd * AUG_DIM:(hd + 1) * AUG_DIM]
        s = _dot_nt(kblk, qp_ref[:, hd * AUG_DIM:(hd + 1) * AUG_DIM])
        if masked:
            krow = lax.broadcasted_iota(jnp.int32, (blk, blk), 0)
            qcol = lax.broadcasted_iota(jnp.int32, (blk, blk), 1)
            s = jnp.where(krow <= qcol, s, -jnp.inf)
        m_old = m_ref[hd]
        m_new = jnp.maximum(m_old, jnp.max(s, axis=0, keepdims=True))
        alpha = jnp.exp2(m_old - m_new)
        p = jnp.exp2(s - m_new)
        l_ref[hd] = alpha * l_ref[hd] + jnp.sum(p, axis=0, keepdims=True)
        m_ref[hd] = m_new
        vt = vt_ref[j, hd * HEAD_DIM:(hd + 1) * HEAD_DIM, :]
        acc_ref[hd] = alpha * acc_ref[hd] + _dot(vt, p.astype(BF16))


def _out_kernel(x_ref, ma_ref, mls_ref, wo_ref, fg_ref, y_ref, *, final_norm):
    y = x_ref[...] + _dot(ma_ref[...].astype(BF16), wo_ref[0:ATT_WIDTH, :])
    y = y + _dot(mls_ref[...], wo_ref[ATT_WIDTH:, :])
    if final_norm:
        y = y * _rms_scale(y) * fg_ref[...]
    y_ref[...] = y


def _out_proj(x2, ma2, mls2, wo, fg, *, rows, final_norm):
    n, d = x2.shape
    spec = lambda w: pl.BlockSpec((rows, w), lambda i: (i, 0))
    return pl.pallas_call(
        functools.partial(_out_kernel, final_norm=final_norm),
        out_shape=jax.ShapeDtypeStruct((n, d), F32),
        grid=(n // rows,),
        in_specs=[spec(d), spec(ATT_WIDTH), spec(LRU_WIDTH + SC_WIDTH),
                  _const_spec(wo.shape), _const_spec(fg.shape)],
        out_specs=spec(d),
        compiler_params=pltpu.CompilerParams(
            dimension_semantics=("arbitrary",), vmem_limit_bytes=VMEM_LIMIT_BYTES),
        name="out_proj",
    )(x2, ma2, mls2, wo, fg)


def _sample_in_kernel(x_ref, g_ref, wqkvg_ref, wrest_ref, wfl_ref, bf_ref, cw_ref, cb_ref,
                      wa_ref, ba_ref, wx_ref, bx_ref, lam_ref, scw_ref, h0_ref, rgbuf_ref, scbuf_ref,
                      q_ref, k_ref, v_ref, gsa_ref, mls_ref, logf_ref, rgh_ref, rgc_ref, scc_ref):
    x = x_ref[...]
    h = (x * _rms_scale(x) * g_ref[...]).astype(BF16)
    q_ref[...] = _dot(h, wqkvg_ref[:, 0:ATT_WIDTH]) * (HEAD_DIM ** -0.5)
    k_ref[...] = _dot(h, wqkvg_ref[:, ATT_WIDTH:2 * ATT_WIDTH])
    v_ref[...] = _dot(h, wqkvg_ref[:, 2 * ATT_WIDTH:3 * ATT_WIDTH])
    gsa_ref[...] = _silu(_dot(h, wqkvg_ref[:, 3 * ATT_WIDTH:4 * ATT_WIDTH]))
    logf = _log_sigmoid(_dot(h, wfl_ref[...]) + bf_ref[...])
    logf_ref[...] = logf[:, 0:N_HEADS]

    xr = _dot(h, wrest_ref[:, 0:LRU_WIDTH])
    gr = _dot(h, wrest_ref[:, LRU_WIDTH:2 * LRU_WIDTH])
    xc = cb_ref[...] + cw_ref[LRU_CONV - 1:LRU_CONV, :] * xr
    for j in range(LRU_CONV - 1):
        xc = xc + cw_ref[j:j + 1, :] * rgbuf_ref[j]
    for j in range(LRU_CONV - 2):
        rgc_ref[j] = rgbuf_ref[j + 1]
    rgc_ref[LRU_CONV - 2] = xr
    a, drive = _lru_gates(xc, wa_ref[...], ba_ref[...], wx_ref[...], bx_ref[...], lam_ref[...])
    hn = a * h0_ref[...] + drive
    rgh_ref[...] = hn
    mls_ref[:, 0:LRU_WIDTH] = (hn * _silu(gr)).astype(BF16)

    o0 = 2 * LRU_WIDTH
    bc = _dot(h, wrest_ref[:, o0:o0 + SC_WIDTH])
    cc = _dot(h, wrest_ref[:, o0 + SC_WIDTH:o0 + 2 * SC_WIDTH])
    hc = _dot(h, wrest_ref[:, o0 + 2 * SC_WIDTH:o0 + 3 * SC_WIDTH])
    gc = _dot(h, wrest_ref[:, o0 + 3 * SC_WIDTH:o0 + 4 * SC_WIDTH])
    u = cc * hc
    cu = scw_ref[SC_CONV - 1:SC_CONV, :] * u
    for j in range(SC_CONV - 1):
        cu = cu + scw_ref[j:j + 1, :] * scbuf_ref[j]
    for j in range(SC_CONV - 2):
        scc_ref[j] = scbuf_ref[j + 1]
    scc_ref[SC_CONV - 2] = u
    mls_ref[:, LRU_WIDTH:LRU_WIDTH + SC_WIDTH] = (bc * cu * _silu(gc)).astype(BF16)


def _sample_in(x2, lw, h0, rgbuf_t, scbuf_t):
    n, _ = x2.shape
    weights = (lw["g"], lw["w_qkvg"], lw["w_rest"], lw["w_fl"], lw["b_f"], lw["rg_conv_w"],
               lw["rg_conv_b"], lw["wa"], lw["ba"], lw["wx"], lw["bx"], lw["lam"], lw["sc_conv_w"])
    out_shape = (
        jax.ShapeDtypeStruct((n, ATT_WIDTH), F32),
        jax.ShapeDtypeStruct((n, ATT_WIDTH), F32),
        jax.ShapeDtypeStruct((n, ATT_WIDTH), F32),
        jax.ShapeDtypeStruct((n, ATT_WIDTH), F32),
        jax.ShapeDtypeStruct((n, LRU_WIDTH + SC_WIDTH), BF16),
        jax.ShapeDtypeStruct((n, N_HEADS), F32),
        jax.ShapeDtypeStruct((n, LRU_WIDTH), F32),
        jax.ShapeDtypeStruct((LRU_CONV - 1, n, LRU_WIDTH), F32),
        jax.ShapeDtypeStruct((SC_CONV - 1, n, SC_WIDTH), F32),
    )
    operands = (x2,) + weights + (h0, rgbuf_t, scbuf_t)
    full = lambda shape: pl.BlockSpec(shape, lambda i, nd=len(shape): (0,) * nd)
    return pl.pallas_call(
        _sample_in_kernel,
        out_shape=out_shape,
        grid=(1,),
        in_specs=[_const_spec(o.shape) for o in operands],
        out_specs=tuple(full(o.shape) for o in out_shape),
        compiler_params=pltpu.CompilerParams(
            dimension_semantics=("arbitrary",), vmem_limit_bytes=VMEM_LIMIT_BYTES),
        name="sample_in",
    )(*operands)


def _decode_step(j, n_steps, q_ref, kn_ref, vn_ref, fn_ref, gsa_ref, page_refs, o_ref, scratch):
    npg = DEC_PAGES
    k_refs = page_refs[0:npg]
    v_refs = page_refs[npg:2 * npg]
    f_refs = page_refs[2 * npg:3 * npg]
    qb_ref, m_ref, l_ref, acc_ref, tot_ref = scratch
    page = LANES
    width = page * N_HEADS

    @pl.when(j == 0)
    def _():
        q = q_ref[...]
        qb_ref[...] = q.astype(BF16)
        m_ref[...] = jnp.sum(q * kn_ref[...], axis=1, keepdims=True)
        l_ref[...] = jnp.ones_like(l_ref)
        acc_ref[...] = vn_ref[...]
        tot_ref[...] = jnp.broadcast_to(fn_ref[...], (N_HEADS, LANES))

    ri = lax.broadcasted_iota(jnp.int32, (page, 2 * page), 0)
    ci = lax.broadcasted_iota(jnp.int32, (page, 2 * page), 1)
    u2 = jnp.where((ri > ci) | (ci >= page), 1.0, 0.0).astype(F32)
    rs = lax.broadcasted_iota(jnp.int32, (page, width), 0)
    cs = lax.broadcasted_iota(jnp.int32, (page, width), 1)
    spread = jnp.where(cs // N_HEADS == rs, 1.0, 0.0).astype(BF16)
    own = (lax.broadcasted_iota(jnp.int32, (N_HEADS, width), 1) % N_HEADS
           == lax.broadcasted_iota(jnp.int32, (N_HEADS, width), 0))

    tot = tot_ref[...]
    logf_pages = jnp.concatenate([f_refs[i][...] for i in range(npg)], axis=0)
    suf = jnp.dot(logf_pages, u2, precision=lax.Precision.HIGHEST, preferred_element_type=F32)
    bias = []
    for i in range(npg):
        rows_i = slice(i * N_HEADS, (i + 1) * N_HEADS)
        bias.append(tot + suf[rows_i, 0:page])
        tot = tot + suf[rows_i, page:2 * page]
    tot_ref[...] = tot
    hi, mid, lo = _split3(jnp.concatenate(bias, axis=0))
    parts = jnp.concatenate([hi, mid, lo], axis=0).astype(BF16)
    spread_parts = _dot(parts, spread)
    rows = npg * N_HEADS
    bias_w = spread_parts[0:rows] + spread_parts[rows:2 * rows] + spread_parts[2 * rows:3 * rows]

    qb = qb_ref[...]
    scores = []
    for i in range(npg):
        kpage = k_refs[i][...].reshape(width, HEAD_DIM).astype(BF16)
        s = _dot_nt(qb, kpage) + bias_w[i * N_HEADS:(i + 1) * N_HEADS]
        scores.append(jnp.where(own, s, -jnp.inf))
    s_all = jnp.concatenate(scores, axis=1)
    m_old = m_ref[...]
    m_new = jnp.maximum(m_old, jnp.max(s_all, axis=1, keepdims=True))
    alpha = jnp.exp(m_old - m_new)
    p = jnp.exp(s_all - m_new)
    l_ref[...] = alpha * l_ref[...] + jnp.sum(p, axis=1, keepdims=True)
    m_ref[...] = m_new
    pb = p.astype(BF16)
    acc = alpha * acc_ref[...]
    for i in range(npg):
        vpage = v_refs[i][...].reshape(width, HEAD_DIM).astype(BF16)
        acc = acc + _dot(pb[:, i * width:(i + 1) * width], vpage)
    acc_ref[...] = acc

    @pl.when(j == n_steps - 1)
    def _():
        o_ref[...] = acc / l_ref[...] * gsa_ref[...]


def _attn_decode_kernel(pt_ref, qp_ref, kp_ref, vt_ref, gsa_ref, q_ref, kn_ref, vn_ref, fn_ref,
                        gsd_ref, *refs):
    npg = DEC_PAGES
    page_refs = refs[0:3 * npg]
    o_ref, od_ref = refs[3 * npg:3 * npg + 2]
    m_ref, l_ref, acc_ref = refs[3 * npg + 2:3 * npg + 5]
    dec_scratch = refs[3 * npg + 5:]
    qi = pl.program_id(1)
    c = pl.program_id(2)

    @pl.when(c == 0)
    def _():
        m_ref[...] = jnp.full(m_ref.shape, -jnp.inf, F32)
        l_ref[...] = jnp.zeros(l_ref.shape, F32)
        acc_ref[...] = jnp.zeros(acc_ref.shape, F32)

    @pl.when(c < qi)
    def _():
        _attn_kv_block(qp_ref, kp_ref, vt_ref, m_ref, l_ref, acc_ref, c, False)

    @pl.when(c == qi)
    def _():
        _attn_kv_block(qp_ref, kp_ref, vt_ref, m_ref, l_ref, acc_ref, c, True)
        for hd in range(N_HEADS):
            lo, hi = hd * HEAD_DIM, (hd + 1) * HEAD_DIM
            att = (acc_ref[hd] / l_ref[hd]).T
            o_ref[:, lo:hi] = (att * gsa_ref[:, lo:hi].astype(F32)).astype(BF16)

    _decode_step(c, pl.num_programs(2), q_ref, kn_ref, vn_ref, fn_ref, gsd_ref,
                 page_refs, od_ref, dec_scratch)


def _attn_decode(layer, page_table, qp, kp, vt, gsa, cache_k, cache_v, cache_logf_t,
                 q3, kn3, vn3, fn3, gsa3):
    b, s, _ = qp.shape
    blk = ATTN_BLOCK
    nq = s // blk
    n, n_pages = page_table.shape
    page = cache_k.shape[2]
    steps = n_pages // DEC_PAGES
    assert page == LANES and n_pages % DEC_PAGES == 0 and n == b * nq and steps >= nq
    qspec = pl.BlockSpec((None, blk, N_HEADS * AUG_DIM), lambda bi, qi, c, pt: (bi, qi, 0))
    ospec = pl.BlockSpec((None, blk, ATT_WIDTH), lambda bi, qi, c, pt: (bi, qi, 0))
    tok = pl.BlockSpec((None, N_HEADS, HEAD_DIM), lambda bi, qi, c, pt: (bi * nq + qi, 0, 0))

    def page_spec(i, tail):
        def imap(bi, qi, c, pt):
            return (layer, pt[bi * nq + qi, n_pages - 1 - (c * DEC_PAGES + i)]) + (0,) * len(tail)
        return pl.BlockSpec((None, None) + tail, imap)

    kv_tail = (page, N_HEADS, HEAD_DIM)
    per_row = dict(pipeline_mode=pl.Buffered(1))
    in_specs = ([qspec,
                 pl.BlockSpec((None, s, N_HEADS * AUG_DIM), lambda bi, qi, c, pt: (bi, 0, 0),
                              **per_row),
                 pl.BlockSpec((None, nq, ATT_WIDTH, blk), lambda bi, qi, c, pt: (bi, 0, 0, 0),
                              **per_row),
                 ospec,
                 tok, tok, tok,
                 pl.BlockSpec((None, N_HEADS, 1), lambda bi, qi, c, pt: (bi * nq + qi, 0, 0)), tok]
                + [page_spec(i, kv_tail) for i in range(DEC_PAGES)]
                + [page_spec(i, kv_tail) for i in range(DEC_PAGES)]
                + [page_spec(i, (N_HEADS, page)) for i in range(DEC_PAGES)])
    grid_spec = pltpu.PrefetchScalarGridSpec(
        num_scalar_prefetch=1,
        grid=(b, nq, steps),
        in_specs=in_specs,
        out_specs=(ospec, tok),
        scratch_shapes=[
            pltpu.VMEM((N_HEADS, 1, blk), F32),
            pltpu.VMEM((N_HEADS, 1, blk), F32),
            pltpu.VMEM((N_HEADS, HEAD_DIM, blk), F32),
            pltpu.VMEM((N_HEADS, HEAD_DIM), BF16),
            pltpu.VMEM((N_HEADS, 1), F32),
            pltpu.VMEM((N_HEADS, 1), F32),
            pltpu.VMEM((N_HEADS, HEAD_DIM), F32),
            pltpu.VMEM((N_HEADS, LANES), F32),
        ],
    )
    return pl.pallas_call(
        _attn_decode_kernel,
        out_shape=(jax.ShapeDtypeStruct((b, s, ATT_WIDTH), BF16),
                   jax.ShapeDtypeStruct((n, N_HEADS, HEAD_DIM), F32)),
        grid_spec=grid_spec,
        compiler_params=pltpu.CompilerParams(
            dimension_semantics=("arbitrary", "arbitrary", "arbitrary"),
            vmem_limit_bytes=VMEM_LIMIT_BYTES),
        name="attn_decode",
    )(page_table, qp, kp, vt, gsa, q3, kn3, vn3, fn3, gsa3,
      *([cache_k] * DEC_PAGES), *([cache_v] * DEC_PAGES), *([cache_logf_t] * DEC_PAGES))


def _block_diag(w):
    g, i, j = w.shape
    eye = jnp.eye(g, dtype=w.dtype)
    return (eye[:, None, :, None] * w[:, :, None, :]).reshape(g * i, g * j)


def _layer_weights(l, norm_g, w_in, b_f, rg_conv_w, rg_conv_b, rg_wa, rg_ba, rg_wx, rg_bx,
                   rg_lambda, sc_conv_w):
    w = w_in[l]
    row = lambda a: a.reshape(1, -1).astype(F32)
    return dict(
        g=row(norm_g[l]),
        w_qkvg=jnp.concatenate([w[:, 0:_OFF_FL], w[:, _OFF_GA:_OFF_XR]], axis=1).astype(BF16),
        w_rest=w[:, _OFF_XR:].astype(BF16),
        w_fl=jnp.pad(w[:, _OFF_FL:_OFF_GA], ((0, 0), (0, LANES - N_HEADS))).astype(BF16),
        b_f=jnp.pad(row(b_f[l]), ((0, 0), (0, LANES - N_HEADS))),
        rg_conv_w=rg_conv_w[l].astype(F32), rg_conv_b=row(rg_conv_b[l]),
        wa=_block_diag(rg_wa[l]).astype(BF16), ba=row(rg_ba[l]),
        wx=_block_diag(rg_wx[l]).astype(BF16), bx=row(rg_bx[l]),
        lam=row(rg_lambda[l]), sc_conv_w=sc_conv_w[l].astype(F32),
    )


def kernel(x_prompt, x_sample, cache_k, cache_v, cache_logf, state_rg_h, state_rg_conv, state_sc_conv, page_table, norm_g, w_in, b_f, rg_conv_w, rg_conv_b, rg_wa, rg_ba, rg_wx, rg_bx, rg_lambda, sc_conv_w, w_out, final_g):
    depth = w_in.shape[0]
    b, s, d = x_prompt.shape
    n = x_sample.shape[0]
    cache_logf_t = jnp.swapaxes(cache_logf, 2, 3)
    fg = final_g.reshape(1, d).astype(F32)
    sel = _bias_feature_matrix()

    yp = x_prompt
    ys = x_sample.reshape(n, d)
    sp, ss = [], []
    kv_state = None
    for l in range(depth):
        lw = _layer_weights(l, norm_g, w_in, b_f, rg_conv_w, rg_conv_b, rg_wa, rg_ba, rg_wx,
                            rg_bx, rg_lambda, sc_conv_w)
        wo = w_out[l].astype(BF16)
        last = l == depth - 1

        qp, kp, k_all, v_all, vt, gsa, mls, logf, rgh, rgc, scc = _prompt_in(
            yp, lw, sel, l, depth, kv_state)
        kv_state = (k_all, v_all)
        qs, ks, vs, gsas, mlss, logfs, rghs, rgcs, sccs = _sample_in(
            ys, lw, state_rg_h[l], jnp.swapaxes(state_rg_conv[l], 0, 1),
            jnp.swapaxes(state_sc_conv[l], 0, 1))
        heads = lambda a: a.reshape(n, N_HEADS, HEAD_DIM)
        ma, mas = _attn_decode(l, page_table, qp, kp, vt, gsa, cache_k, cache_v, cache_logf_t,
                               heads(qs), heads(ks), heads(vs), logfs.reshape(n, N_HEADS, 1),
                               heads(gsas))
        yp = _out_proj(yp.reshape(b * s, d), ma.reshape(b * s, ATT_WIDTH),
                       mls.reshape(b * s, LRU_WIDTH + SC_WIDTH), wo, fg,
                       rows=OUT_ROWS, final_norm=last).reshape(b, s, d)
        sp.append((logf, rgh.reshape(b, LRU_WIDTH), rgc, scc))
        ys = _out_proj(ys, mas.reshape(n, ATT_WIDTH), mlss, wo, fg, rows=n, final_norm=last)
        ss.append((ks.reshape(n, 1, N_HEADS, HEAD_DIM), vs.reshape(n, 1, N_HEADS, HEAD_DIM),
                   logfs.reshape(n, 1, N_HEADS), rghs, jnp.swapaxes(rgcs, 0, 1),
                   jnp.swapaxes(sccs, 0, 1)))

    stack = lambda states, i: jnp.stack([st[i] for st in states])
    return (yp, ys.reshape(n, 1, d),
            kv_state[0], kv_state[1], stack(sp, 0), stack(sp, 1), stack(sp, 2), stack(sp, 3),
            stack(ss, 0), stack(ss, 1), stack(ss, 2), stack(ss, 3), stack(ss, 4), stack(ss, 5))
```

```python
import functools

import numpy as np
import jax
import jax.numpy as jnp
from jax import lax
from jax.experimental import pallas as pl
from jax.experimental.pallas import tpu as pltpu

F32 = jnp.float32
BF16 = jnp.bfloat16

N_HEADS = 8
HEAD_DIM = 128
ATT_WIDTH = N_HEADS * HEAD_DIM
LRU_WIDTH = 512
LRU_BLOCKS = 8
LRU_CONV = 4
LRU_C = 8.0
SC_WIDTH = 512
SC_CONV = 3
EPS = 1e-6
_OFF_FL = 3 * ATT_WIDTH
_OFF_GA = _OFF_FL + N_HEADS
_OFF_XR = _OFF_GA + ATT_WIDTH

LANES = 128
SUBLANES = 8
VMEM_LIMIT_BYTES = 60 * 1024 * 1024

PROJ_ROWS = 512
ATTN_BLOCK = 512
OUT_ROWS = 512
DEC_PAGES = 16
PROJ_COLS = 2 * HEAD_DIM

LOG2E = 1.4426950408889634
AUG_DIM = 2 * HEAD_DIM
_SPLIT_LANES = (0, 16, 32)
_ONES_LANE = N_HEADS


def _sigmoid(x):
    return 1.0 / (1.0 + jnp.exp(-x))


def _silu(x):
    return x * _sigmoid(x)


def _log_sigmoid(x):
    return jnp.minimum(x, 0.0) - jnp.log1p(jnp.exp(-jnp.abs(x)))


def _softplus(x):
    return jnp.maximum(x, 0.0) + jnp.log1p(jnp.exp(-jnp.abs(x)))


def _rms_scale(x):
    return lax.rsqrt(jnp.mean(x * x, axis=-1, keepdims=True) + EPS)


def _dot(a, b):
    return jnp.dot(a, b, preferred_element_type=F32)


def _dot_nt(a, b):
    return lax.dot_general(a, b, (((1,), (1,)), ((), ())), preferred_element_type=F32)


def _split3(x):
    hi = x.astype(BF16).astype(F32)
    r = x - hi
    mid = r.astype(BF16).astype(F32)
    lo = (r - mid).astype(BF16).astype(F32)
    return hi, mid, lo


def _lru_gates(xc, wa, ba, wx, bx, lam):
    xb = xc.astype(BF16)
    r = _sigmoid(_dot(xb, wa) + ba)
    i = _sigmoid(_dot(xb, wx) + bx)
    a = jnp.exp((-LRU_C) * r * _softplus(-lam))
    drive = jnp.sqrt(1.0 - a * a) * (i * xc)
    return a, drive


def _bias_feature_matrix():
    m = np.zeros((LANES, 2 * ATT_WIDTH), np.float32)
    for h in range(N_HEADS):
        for part, lane0 in enumerate(_SPLIT_LANES):
            m[lane0 + h, h * HEAD_DIM + part] = 1.0
            m[lane0 + h, ATT_WIDTH + h * HEAD_DIM + 3 + part] = -1.0
            m[_ONES_LANE, h * HEAD_DIM + 3 + part] = 1.0
            m[_ONES_LANE, ATT_WIDTH + h * HEAD_DIM + part] = 1.0
    return jnp.asarray(m, BF16)


def _prompt_in_kernel(x_ref, g_ref, wqkvg_ref, wrest_ref, wfl_ref, bf_ref, sel_ref, cw_ref, cb_ref,
                      wa_ref, ba_ref, wx_ref, bx_ref, lam_ref, scw_ref, *refs, rows, n_state_in):
    (qp_ref, kp_ref, k_ref, v_ref, vt_ref, gsa_ref, mls_ref, logf_ref, rgh_ref, rgc_ref, scc_ref,
     ext_rg, ext_sc, hcar_ref, ccar_ref) = refs[n_state_in:]
    t = pl.program_id(1)
    last_t = pl.num_programs(1) - 1

    @pl.when(t == 0)
    def _():
        ext_rg[0:SUBLANES, :] = jnp.zeros((SUBLANES, LRU_WIDTH), F32)
        ext_sc[0:SUBLANES, :] = jnp.zeros((SUBLANES, SC_WIDTH), F32)
        hcar_ref[...] = jnp.zeros_like(hcar_ref)
        ccar_ref[...] = jnp.zeros_like(ccar_ref)

    x = x_ref[...]
    h = (x * _rms_scale(x) * g_ref[...]).astype(BF16)

    logf = _log_sigmoid(_dot(h, wfl_ref[...]) + bf_ref[...])
    logf_ref[...] = logf[:, 0:N_HEADS]
    row = lax.broadcasted_iota(jnp.int32, (rows, LANES), 0)
    lane = lax.broadcasted_iota(jnp.int32, (rows, LANES), 1)
    cs = logf
    shift = 1
    while shift < rows:
        cs = cs + jnp.where(row >= shift, pltpu.roll(cs, shift, 0), 0.0)
        shift *= 2
    cs = cs + ccar_ref[...]
    ccar_ref[...] = cs[rows - 1:rows, :]
    cx = jnp.where(lane < N_HEADS, cs * LOG2E, jnp.where(lane == _ONES_LANE, 1.0, 0.0))
    hi, mid, lo = _split3(cx)
    packed = (hi + pltpu.roll(mid, _SPLIT_LANES[1], 1)
              + pltpu.roll(lo, _SPLIT_LANES[2], 1)).astype(BF16)

    ch = PROJ_COLS
    heads_per_chunk = ch // HEAD_DIM

    def attn_chunk(c):
        lo_c, hi_c = c * ch, (c + 1) * ch
        q = _dot(h, wqkvg_ref[:, lo_c:hi_c]) * (HEAD_DIM ** -0.5 * LOG2E)
        k = _dot(h, wqkvg_ref[:, ATT_WIDTH + lo_c:ATT_WIDTH + hi_c])
        v = _dot(h, wqkvg_ref[:, 2 * ATT_WIDTH + lo_c:2 * ATT_WIDTH + hi_c])
        ga = _dot(h, wqkvg_ref[:, 3 * ATT_WIDTH + lo_c:3 * ATT_WIDTH + hi_c])
        augq = _dot(packed, sel_ref[:, lo_c:hi_c])
        augk = _dot(packed, sel_ref[:, ATT_WIDTH + lo_c:ATT_WIDTH + hi_c])
        vt_ref[lo_c:hi_c, :] = v.T.astype(BF16)
        gsa_ref[:, lo_c:hi_c] = _silu(ga).astype(BF16)
        for hh in range(heads_per_chunk):
            hd = c * heads_per_chunk + hh
            src = slice(hh * HEAD_DIM, (hh + 1) * HEAD_DIM)
            qp_ref[:, hd * AUG_DIM:hd * AUG_DIM + HEAD_DIM] = q[:, src].astype(BF16)
            qp_ref[:, hd * AUG_DIM + HEAD_DIM:(hd + 1) * AUG_DIM] = augq[:, src].astype(BF16)
            kp_ref[:, hd * AUG_DIM:hd * AUG_DIM + HEAD_DIM] = k[:, src].astype(BF16)
            kp_ref[:, hd * AUG_DIM + HEAD_DIM:(hd + 1) * AUG_DIM] = augk[:, src].astype(BF16)
        return k, v

    rmod =lax.broadcasted_iota(jnp.int32, (rows, ch), 0) % SUBLANES

    def lru_chunk(c):
        cols = slice(c * ch, (c + 1) * ch)
        xr = _dot(h, wrest_ref[:, c * ch:(c + 1) * ch])
        gr = _dot(h, wrest_ref[:, LRU_WIDTH + c * ch:LRU_WIDTH + (c + 1) * ch])
        ext_rg[SUBLANES:SUBLANES + rows, cols] = xr
        xc = cb_ref[:, cols] + cw_ref[LRU_CONV - 1:LRU_CONV, cols] * xr
        for j in range(LRU_CONV - 1):
            o = SUBLANES - (LRU_CONV - 1) + j
            xc = xc + cw_ref[j:j + 1, cols] * ext_rg[o:o + rows, cols]
        tail_c = ext_rg[rows:rows + SUBLANES, cols]
        ext_rg[0:SUBLANES, cols] = tail_c
        a, drive = _lru_gates(xc, wa_ref[cols, cols], ba_ref[:, cols], wx_ref[cols, cols],
                              bx_ref[:, cols], lam_ref[:, cols])
        shift = 1
        while shift < SUBLANES:
            keep = rmod >= shift
            a_sh = pltpu.roll(a, shift, 0)
            d_sh = pltpu.roll(drive, shift, 0)
            drive = jnp.where(keep, a * d_sh + drive, drive)
            a = jnp.where(keep, a * a_sh, a)
            shift *= 2
        hprev = hcar_ref[:, cols]
        groups = []
        for gi in range(rows // SUBLANES):
            sl = slice(gi * SUBLANES, (gi + 1) * SUBLANES)
            hg = a[sl, :] * hprev + drive[sl, :]
            groups.append(hg)
            hprev = hg[SUBLANES - 1:SUBLANES, :]
        hcar_ref[:, cols] = hprev
        lru = jnp.concatenate(groups, axis=0)
        mls_ref[:, cols] = (lru * _silu(gr)).astype(BF16)
        return hprev, tail_c

    def conv_chunk(c):
        cols = slice(c * ch, (c + 1) * ch)
        o0 = 2 * LRU_WIDTH + c * ch
        bc = _dot(h, wrest_ref[:, o0:o0 + ch])
        cc = _dot(h, wrest_ref[:, o0 + SC_WIDTH:o0 + SC_WIDTH + ch])
        hc = _dot(h, wrest_ref[:, o0 + 2 * SC_WIDTH:o0 + 2 * SC_WIDTH + ch])
        gc = _dot(h, wrest_ref[:, o0 + 3 * SC_WIDTH:o0 + 3 * SC_WIDTH + ch])
        u = cc * hc
        ext_sc[SUBLANES:SUBLANES + rows, cols] = u
        cu = scw_ref[SC_CONV - 1:SC_CONV, cols] * u
        for j in range(SC_CONV - 1):
            o = SUBLANES - (SC_CONV - 1) + j
            cu = cu + scw_ref[j:j + 1, cols] * ext_sc[o:o + rows, cols]
        tail_c = ext_sc[rows:rows + SUBLANES, cols]
        ext_sc[0:SUBLANES, cols] = tail_c
        mls_ref[:, LRU_WIDTH + c * ch:LRU_WIDTH + (c + 1) * ch] = (bc * cu * _silu(gc)).astype(BF16)
        return tail_c

    kv = [attn_chunk(c) for c in range(ATT_WIDTH // ch)]
    k_ref[...] = jnp.concatenate([c[0] for c in kv], axis=1).reshape(rows, N_HEADS, HEAD_DIM)
    v_ref[...] = jnp.concatenate([c[1] for c in kv], axis=1).reshape(rows, N_HEADS, HEAD_DIM)
    lru_out = [lru_chunk(c) for c in range(LRU_WIDTH // ch)]
    hlast = jnp.concatenate([o[0] for o in lru_out], axis=1)
    tail = jnp.concatenate([o[1] for o in lru_out], axis=1)
    tail_sc = jnp.concatenate([conv_chunk(c) for c in range(SC_WIDTH // ch)], axis=1)

    @pl.when(t == last_t)
    def _():
        rgh_ref[...] = hlast
        rgc_ref[...] = tail[SUBLANES - (LRU_CONV - 1):SUBLANES, :]
        scc_ref[...] = tail_sc[SUBLANES - (SC_CONV - 1):SUBLANES, :]


def _const_spec(shape):
    nd = len(shape)
    return pl.BlockSpec(shape, lambda *_: (0,) * nd, pipeline_mode=pl.Buffered(1))


def _prompt_in(x, lw, sel, layer, depth, kv_state):
    b, s, d = x.shape
    rows = PROJ_ROWS
    assert rows == ATTN_BLOCK and s % rows == 0
    nt = s // rows
    row_spec = lambda w: pl.BlockSpec((None, rows, w), lambda bi, ti: (bi, ti, 0))
    last_spec = lambda r, w: pl.BlockSpec((None, r, w), lambda bi, ti: (bi, 0, 0))
    state_spec = pl.BlockSpec((None, None, rows, N_HEADS, HEAD_DIM),
                              lambda bi, ti: (layer, bi, ti, 0, 0))
    weights = (lw["g"], lw["w_qkvg"], lw["w_rest"], lw["w_fl"], lw["b_f"], sel, lw["rg_conv_w"],
               lw["rg_conv_b"], lw["wa"], lw["ba"], lw["wx"], lw["bx"], lw["lam"], lw["sc_conv_w"])
    state_in = () if kv_state is None else tuple(kv_state)
    n_in = 1 + len(weights)
    aliases = {n_in + i: 2 + i for i in range(len(state_in))}
    out_shape = (
        jax.ShapeDtypeStruct((b, s, N_HEADS * AUG_DIM), BF16),
        jax.ShapeDtypeStruct((b, s, N_HEADS * AUG_DIM), BF16),
        jax.ShapeDtypeStruct((depth, b, s, N_HEADS, HEAD_DIM), F32),
        jax.ShapeDtypeStruct((depth, b, s, N_HEADS, HEAD_DIM), F32),
        jax.ShapeDtypeStruct((b, nt, ATT_WIDTH, rows), BF16),
        jax.ShapeDtypeStruct((b, s, ATT_WIDTH), BF16),
        jax.ShapeDtypeStruct((b, s, LRU_WIDTH + SC_WIDTH), BF16),
        jax.ShapeDtypeStruct((b, s, N_HEADS), F32),
        jax.ShapeDtypeStruct((b, 1, LRU_WIDTH), F32),
        jax.ShapeDtypeStruct((b, LRU_CONV - 1, LRU_WIDTH), F32),
        jax.ShapeDtypeStruct((b, SC_CONV - 1, SC_WIDTH), F32),
    )
    out_specs = (
        row_spec(N_HEADS * AUG_DIM), row_spec(N_HEADS * AUG_DIM),
        state_spec, state_spec,
        pl.BlockSpec((None, None, ATT_WIDTH, rows), lambda bi, ti: (bi, ti, 0, 0)),
        row_spec(ATT_WIDTH), row_spec(LRU_WIDTH + SC_WIDTH), row_spec(N_HEADS),
        last_spec(1, LRU_WIDTH), last_spec(LRU_CONV - 1, LRU_WIDTH), last_spec(SC_CONV - 1, SC_WIDTH),
    )
    return pl.pallas_call(
        functools.partial(_prompt_in_kernel, rows=rows, n_state_in=len(state_in)),
        out_shape=out_shape,
        grid=(b, nt),
        in_specs=([row_spec(d)] + [_const_spec(w.shape) for w in weights]
                  + [pl.BlockSpec(memory_space=pl.ANY)] * len(state_in)),
        out_specs=out_specs,
        scratch_shapes=[
            pltpu.VMEM((rows + SUBLANES, LRU_WIDTH), F32),
            pltpu.VMEM((rows + SUBLANES, SC_WIDTH), F32),
            pltpu.VMEM((1, LRU_WIDTH), F32),
            pltpu.VMEM((1, LANES), F32),
        ],
        input_output_aliases=aliases,
        compiler_params=pltpu.CompilerParams(
            dimension_semantics=("arbitrary", "arbitrary"), vmem_limit_bytes=VMEM_LIMIT_BYTES),
        name="prompt_in",
    )(x, *weights, *state_in)


def _attn_kv_block(qp_ref, kp_ref, vt_ref, m_ref, l_ref, acc_ref, j, masked):
    blk = ATTN_BLOCK
    r0 = pl.multiple_of(j * blk, blk)
    for hd in range(N_HEADS):
        kblk = kp_ref[pl.ds(r0, blk), hd * AUG_DIM:(hd + 1) * AUG_DIM]
        s = _dot_nt(kblk, qp_ref[:, hd * AUG_DIM:(hd + 1) * AUG_DIM])
        if masked:
            krow = lax.broadcasted_iota(jnp.int32, (blk, blk), 0)
            qcol = lax.broadcasted_iota(jnp.int32, (blk, blk), 1)
            s = jnp.where(krow <= qcol, s, -jnp.inf)
        m_old = m_ref[hd]
        m_new = jnp.maximum(m_old, jnp.max(s, axis=0, keepdims=True))
        alpha = jnp.exp2(m_old - m_new)
        p = jnp.exp2(s - m_new)
        l_ref[hd] = alpha * l_ref[hd] + jnp.sum(p, axis=0, keepdims=True)
        m_ref[hd] = m_new
        vt = vt_ref[j, hd * HEAD_DIM:(hd + 1) * HEAD_DIM, :]
        acc_ref[hd] = alpha * acc_ref[hd] + _dot(vt, p.astype(BF16))


def _out_kernel(x_ref, ma_ref, mls_ref, wo_ref, fg_ref, y_ref, *, final_norm):
    y = x_ref[...] + _dot(ma_ref[...].astype(BF16), wo_ref[0:ATT_WIDTH, :])
    y = y + _dot(mls_ref[...], wo_ref[ATT_WIDTH:, :])
    if final_norm:
        y = y * _rms_scale(y) * fg_ref[...]
    y_ref[...] = y


def _out_proj(x2, ma2, mls2, wo, fg, *, rows, final_norm):
    n, d = x2.shape
    spec = lambda w: pl.BlockSpec((rows, w), lambda i: (i, 0))
    return pl.pallas_call(
        functools.partial(_out_kernel, final_norm=final_norm),
        out_shape=jax.ShapeDtypeStruct((n, d), F32),
        grid=(n // rows,),
        in_specs=[spec(d), spec(ATT_WIDTH), spec(LRU_WIDTH + SC_WIDTH),
                  _const_spec(wo.shape), _const_spec(fg.shape)],
        out_specs=spec(d),
        compiler_params=pltpu.CompilerParams(
            dimension_semantics=("arbitrary",), vmem_limit_bytes=VMEM_LIMIT_BYTES),
        name="out_proj",
    )(x2, ma2, mls2, wo, fg)


def _sample_in_kernel(x_ref, g_ref, wqkvg_ref, wrest_ref, wfl_ref, bf_ref, cw_ref, cb_ref,
                      wa_ref, ba_ref, wx_ref, bx_ref, lam_ref, scw_ref, h0_ref, rgbuf_ref, scbuf_ref,
                      q_ref, k_ref, v_ref, gsa_ref, mls_ref, logf_ref, rgh_ref, rgc_ref, scc_ref):
    x = x_ref[...]
    h = (x * _rms_scale(x) * g_ref[...]).astype(BF16)
    q_ref[...] = _dot(h, wqkvg_ref[:, 0:ATT_WIDTH]) * (HEAD_DIM ** -0.5)
    k_ref[...] = _dot(h, wqkvg_ref[:, ATT_WIDTH:2 * ATT_WIDTH])
    v_ref[...] = _dot(h, wqkvg_ref[:, 2 * ATT_WIDTH:3 * ATT_WIDTH])
    gsa_ref[...] = _silu(_dot(h, wqkvg_ref[:, 3 * ATT_WIDTH:4 * ATT_WIDTH]))
    logf = _log_sigmoid(_dot(h, wfl_ref[...]) + bf_ref[...])
    logf_ref[...] = logf[:, 0:N_HEADS]

    xr = _dot(h, wrest_ref[:, 0:LRU_WIDTH])
    gr = _dot(h, wrest_ref[:, LRU_WIDTH:2 * LRU_WIDTH])
    xc = cb_ref[...] + cw_ref[LRU_CONV - 1:LRU_CONV, :] * xr
    for j in range(LRU_CONV - 1):
        xc = xc + cw_ref[j:j + 1, :] * rgbuf_ref[j]
    for j in range(LRU_CONV - 2):
        rgc_ref[j] = rgbuf_ref[j + 1]
    rgc_ref[LRU_CONV - 2] = xr
    a, drive = _lru_gates(xc, wa_ref[...], ba_ref[...], wx_ref[...], bx_ref[...], lam_ref[...])
    hn = a * h0_ref[...] + drive
    rgh_ref[...] = hn
    mls_ref[:, 0:LRU_WIDTH] = (hn * _silu(gr)).astype(BF16)

    o0 = 2 * LRU_WIDTH
    bc = _dot(h, wrest_ref[:, o0:o0 + SC_WIDTH])
    cc = _dot(h, wrest_ref[:, o0 + SC_WIDTH:o0 + 2 * SC_WIDTH])
    hc = _dot(h, wrest_ref[:, o0 + 2 * SC_WIDTH:o0 + 3 * SC_WIDTH])
    gc = _dot(h, wrest_ref[:, o0 + 3 * SC_WIDTH:o0 + 4 * SC_WIDTH])
    u = cc * hc
    cu = scw_ref[SC_CONV - 1:SC_CONV, :] * u
    for j in range(SC_CONV - 1):
        cu = cu + scw_ref[j:j + 1, :] * scbuf_ref[j]
    for j in range(SC_CONV - 2):
        scc_ref[j] = scbuf_ref[j + 1]
    scc_ref[SC_CONV - 2] = u
    mls_ref[:, LRU_WIDTH:LRU_WIDTH + SC_WIDTH] = (bc * cu * _silu(gc)).astype(BF16)


def _sample_in(x2, lw, h0, rgbuf_t, scbuf_t):
    n, _ = x2.shape
    weights = (lw["g"], lw["w_qkvg"], lw["w_rest"], lw["w_fl"], lw["b_f"], lw["rg_conv_w"],
               lw["rg_conv_b"], lw["wa"], lw["ba"], lw["wx"], lw["bx"], lw["lam"], lw["sc_conv_w"])
    out_shape = (
        jax.ShapeDtypeStruct((n, ATT_WIDTH), F32),
        jax.ShapeDtypeStruct((n, ATT_WIDTH), F32),
        jax.ShapeDtypeStruct((n, ATT_WIDTH), F32),
        jax.ShapeDtypeStruct((n, ATT_WIDTH), F32),
        jax.ShapeDtypeStruct((n, LRU_WIDTH + SC_WIDTH), BF16),
        jax.ShapeDtypeStruct((n, N_HEADS), F32),
        jax.ShapeDtypeStruct((n, LRU_WIDTH), F32),
        jax.ShapeDtypeStruct((LRU_CONV - 1, n, LRU_WIDTH), F32),
        jax.ShapeDtypeStruct((SC_CONV - 1, n, SC_WIDTH), F32),
    )
    operands = (x2,) + weights + (h0, rgbuf_t, scbuf_t)
    full = lambda shape: pl.BlockSpec(shape, lambda i, nd=len(shape): (0,) * nd)
    return pl.pallas_call(
        _sample_in_kernel,
        out_shape=out_shape,
        grid=(1,),
        in_specs=[_const_spec(o.shape) for o in operands],
        out_specs=tuple(full(o.shape) for o in out_shape),
        compiler_params=pltpu.CompilerParams(
            dimension_semantics=("arbitrary",), vmem_limit_bytes=VMEM_LIMIT_BYTES),
        name="sample_in",
    )(*operands)


def _decode_step(j, n_steps, q_ref, kn_ref, vn_ref, fn_ref, gsa_ref, page_refs, o_ref, scratch):
    npg = DEC_PAGES
    k_refs = page_refs[0:npg]
    v_refs = page_refs[npg:2 * npg]
    f_refs = page_refs[2 * npg:3 * npg]
    qb_ref, m_ref, l_ref, acc_ref, tot_ref = scratch
    page = LANES
    width = page * N_HEADS

    @pl.when(j == 0)
    def _():
        q = q_ref[...]
        qb_ref[...] = q.astype(BF16)
        m_ref[...] = jnp.sum(q * kn_ref[...], axis=1, keepdims=True)
        l_ref[...] = jnp.ones_like(l_ref)
        acc_ref[...] = vn_ref[...]
        tot_ref[...] = jnp.broadcast_to(fn_ref[...], (N_HEADS, LANES))

    ri = lax.broadcasted_iota(jnp.int32, (page, 2 * page), 0)
    ci = lax.broadcasted_iota(jnp.int32, (page, 2 * page), 1)
    u2 = jnp.where((ri > ci) | (ci >= page), 1.0, 0.0).astype(F32)
    rs = lax.broadcasted_iota(jnp.int32, (page, width), 0)
    cs = lax.broadcasted_iota(jnp.int32, (page, width), 1)
    spread = jnp.where(cs // N_HEADS == rs, 1.0, 0.0).astype(BF16)
    own = (lax.broadcasted_iota(jnp.int32, (N_HEADS, width), 1) % N_HEADS
           == lax.broadcasted_iota(jnp.int32, (N_HEADS, width), 0))

    tot = tot_ref[...]
    logf_pages = jnp.concatenate([f_refs[i][...] for i in range(npg)], axis=0)
    suf = jnp.dot(logf_pages, u2, precision=lax.Precision.HIGHEST, preferred_element_type=F32)
    bias = []
    for i in range(npg):
        rows_i = slice(i * N_HEADS, (i + 1) * N_HEADS)
        bias.append(tot + suf[rows_i, 0:page])
        tot = tot + suf[rows_i, page:2 * page]
    tot_ref[...] = tot
    hi, mid, lo = _split3(jnp.concatenate(bias, axis=0))
    parts = jnp.concatenate([hi, mid, lo], axis=0).astype(BF16)
    spread_parts = _dot(parts, spread)
    rows = npg * N_HEADS
    bias_w = spread_parts[0:rows] + spread_parts[rows:2 * rows] + spread_parts[2 * rows:3 * rows]

    qb = qb_ref[...]
    scores = []
    for i in range(npg):
        kpage = k_refs[i][...].reshape(width, HEAD_DIM).astype(BF16)
        s = _dot_nt(qb, kpage) + bias_w[i * N_HEADS:(i + 1) * N_HEADS]
        scores.append(jnp.where(own, s, -jnp.inf))
    s_all = jnp.concatenate(scores, axis=1)
    m_old = m_ref[...]
    m_new = jnp.maximum(m_old, jnp.max(s_all, axis=1, keepdims=True))
    alpha = jnp.exp(m_old - m_new)
    p = jnp.exp(s_all - m_new)
    l_ref[...] = alpha * l_ref[...] + jnp.sum(p, axis=1, keepdims=True)
    m_ref[...] = m_new
    pb = p.astype(BF16)
    acc = alpha * acc_ref[...]
    for i in range(npg):
        vpage = v_refs[i][...].reshape(width, HEAD_DIM).astype(BF16)
        acc = acc + _dot(pb[:, i * width:(i + 1) * width], vpage)
    acc_ref[...] = acc

    @pl.when(j == n_steps - 1)
    def _():
        o_ref[...] = acc / l_ref[...] * gsa_ref[...]


def _attn_decode_kernel(pt_ref, qp_ref, kp_ref, vt_ref, gsa_ref, q_ref, kn_ref, vn_ref, fn_ref,
                        gsd_ref, ck_hbm, cv_hbm, cf_hbm, o_ref, od_ref, m_ref, l_ref, acc_ref,
                        qb_ref, md_ref, ld_ref, accd_ref, tot_ref, kbuf, vbuf, fbuf, sems,
                        *, layer, n_pages):
    npg = DEC_PAGES
    nq = pl.num_programs(1)
    steps = pl.num_programs(2)
    qi = pl.program_id(1)
    c = pl.program_id(2)
    t = (pl.program_id(0) * nq + qi) * steps + c
    total = pl.num_programs(0) * nq * steps
    slot = lax.rem(t, 2)

    def page_copies(step, slot_, page_of):
        seq_ = lax.div(step, steps)
        c_ = lax.rem(step, steps)
        copies = []
        for i in range(npg):
            pid = page_of(seq_, n_pages - 1 - (c_ * npg + i))
            copies.append(pltpu.make_async_copy(
                ck_hbm.at[layer, pid], kbuf.at[slot_, i], sems.at[0, slot_]))
            copies.append(pltpu.make_async_copy(
                cv_hbm.at[layer, pid], vbuf.at[slot_, i], sems.at[1, slot_]))
            copies.append(pltpu.make_async_copy(
                cf_hbm.at[layer, pid], fbuf.at[slot_, i], sems.at[2, slot_]))
        return copies

    def start_pages(step, slot_):
        for cp in page_copies(step, slot_, lambda s_, p_: pt_ref[s_, p_]):
            cp.start()

    @pl.when(t == 0)
    def _():
        start_pages(t, 0)
        start_pages(t + 1, 1)

    @pl.when(c == 0)
    def _():
        m_ref[...] = jnp.full(m_ref.shape, -jnp.inf, F32)
        l_ref[...] = jnp.zeros(l_ref.shape, F32)
        acc_ref[...] = jnp.zeros(acc_ref.shape, F32)

    @pl.when(c < qi)
    def _():
        _attn_kv_block(qp_ref, kp_ref, vt_ref, m_ref, l_ref, acc_ref, c, False)

    @pl.when(c == qi)
    def _():
        _attn_kv_block(qp_ref, kp_ref, vt_ref, m_ref, l_ref, acc_ref, c, True)
        for hd in range(N_HEADS):
            lo, hi = hd * HEAD_DIM, (hd + 1) * HEAD_DIM
            att = (acc_ref[hd] / l_ref[hd]).T
            o_ref[:, lo:hi] = (att * gsa_ref[:, lo:hi].astype(F32)).astype(BF16)

    for cp in page_copies(t, slot, lambda s_, p_: 0):
        cp.wait()
    page_refs = ([kbuf.at[slot, i] for i in range(npg)] + [vbuf.at[slot, i] for i in range(npg)]
                 + [fbuf.at[slot, i] for i in range(npg)])
    _decode_step(c, steps, q_ref, kn_ref, vn_ref, fn_ref, gsd_ref, page_refs, od_ref,
                 (qb_ref, md_ref, ld_ref, accd_ref, tot_ref))

    @pl.when(t + 2 < total)
    def _():
        start_pages(t + 2, slot)


def _attn_decode(layer, page_table, qp, kp, vt, gsa, cache_k, cache_v, cache_logf_t,
                 q3, kn3, vn3, fn3, gsa3):
    b, s, _ = qp.shape
    blk = ATTN_BLOCK
    nq = s // blk
    n, n_pages = page_table.shape
    page = cache_k.shape[2]
    steps = n_pages // DEC_PAGES
    assert page == LANES and n_pages % DEC_PAGES == 0 and n == b * nq and steps >= nq
    assert b * nq * steps >= 2
    qspec = pl.BlockSpec((None, blk, N_HEADS * AUG_DIM), lambda bi, qi, c, pt: (bi, qi, 0))
    ospec = pl.BlockSpec((None, blk, ATT_WIDTH), lambda bi, qi, c, pt: (bi, qi, 0))
    tok = pl.BlockSpec((None, N_HEADS, HEAD_DIM), lambda bi, qi, c, pt: (bi * nq + qi, 0, 0))

    per_row = dict(pipeline_mode=pl.Buffered(1))
    in_hbm = pl.BlockSpec(memory_space=pl.ANY)
    in_specs = [qspec,
                pl.BlockSpec((None, s, N_HEADS * AUG_DIM), lambda bi, qi, c, pt: (bi, 0, 0),
                             **per_row),
                pl.BlockSpec((None, nq, ATT_WIDTH, blk), lambda bi, qi, c, pt: (bi, 0, 0, 0),
                             **per_row),
                ospec,
                tok, tok, tok,
                pl.BlockSpec((None, N_HEADS, 1), lambda bi, qi, c, pt: (bi * nq + qi, 0, 0)), tok,
                in_hbm, in_hbm, in_hbm]
    grid_spec = pltpu.PrefetchScalarGridSpec(
        num_scalar_prefetch=1,
        grid=(b, nq, steps),
        in_specs=in_specs,
        out_specs=(ospec, tok),
        scratch_shapes=[
            pltpu.VMEM((N_HEADS, 1, blk), F32),
            pltpu.VMEM((N_HEADS, 1, blk), F32),
            pltpu.VMEM((N_HEADS, HEAD_DIM, blk), F32),
            pltpu.VMEM((N_HEADS, HEAD_DIM), BF16),
            pltpu.VMEM((N_HEADS, 1), F32),
            pltpu.VMEM((N_HEADS, 1), F32),
            pltpu.VMEM((N_HEADS, HEAD_DIM), F32),
            pltpu.VMEM((N_HEADS, LANES), F32),
            pltpu.VMEM((2, DEC_PAGES, page, N_HEADS, HEAD_DIM), F32),
            pltpu.VMEM((2, DEC_PAGES, page, N_HEADS, HEAD_DIM), F32),
            pltpu.VMEM((2, DEC_PAGES, N_HEADS, page), F32),
            pltpu.SemaphoreType.DMA((3, 2)),
        ],
    )
    return pl.pallas_call(
        functools.partial(_attn_decode_kernel, layer=layer, n_pages=n_pages),
        out_shape=(jax.ShapeDtypeStruct((b, s, ATT_WIDTH), BF16),
                   jax.ShapeDtypeStruct((n, N_HEADS, HEAD_DIM), F32)),
        grid_spec=grid_spec,
        compiler_params=pltpu.CompilerParams(
            dimension_semantics=("arbitrary", "arbitrary", "arbitrary"),
            vmem_limit_bytes=VMEM_LIMIT_BYTES),
        name="attn_decode",
    )(page_table, qp, kp, vt, gsa, q3, kn3, vn3, fn3, gsa3, cache_k, cache_v, cache_logf_t)


def _block_diag(w):
    g, i, j = w.shape
    eye = jnp.eye(g, dtype=w.dtype)
    return (eye[:, None, :, None] * w[:, :, None, :]).reshape(g * i, g * j)


def _layer_weights(l, norm_g, w_in, b_f, rg_conv_w, rg_conv_b, rg_wa, rg_ba, rg_wx, rg_bx,
                   rg_lambda, sc_conv_w):
    w = w_in[l]
    row = lambda a: a.reshape(1, -1).astype(F32)
    return dict(
        g=row(norm_g[l]),
        w_qkvg=jnp.concatenate([w[:, 0:_OFF_FL], w[:, _OFF_GA:_OFF_XR]], axis=1).astype(BF16),
        w_rest=w[:, _OFF_XR:].astype(BF16),
        w_fl=jnp.pad(w[:, _OFF_FL:_OFF_GA], ((0, 0), (0, LANES - N_HEADS))).astype(BF16),
        b_f=jnp.pad(row(b_f[l]), ((0, 0), (0, LANES - N_HEADS))),
        rg_conv_w=rg_conv_w[l].astype(F32), rg_conv_b=row(rg_conv_b[l]),
        wa=_block_diag(rg_wa[l]).astype(BF16), ba=row(rg_ba[l]),
        wx=_block_diag(rg_wx[l]).astype(BF16), bx=row(rg_bx[l]),
        lam=row(rg_lambda[l]), sc_conv_w=sc_conv_w[l].astype(F32),
    )


def kernel(x_prompt, x_sample, cache_k, cache_v, cache_logf, state_rg_h, state_rg_conv, state_sc_conv, page_table, norm_g, w_in, b_f, rg_conv_w, rg_conv_b, rg_wa, rg_ba, rg_wx, rg_bx, rg_lambda, sc_conv_w, w_out, final_g):
    depth = w_in.shape[0]
    b, s, d = x_prompt.shape
    n = x_sample.shape[0]
    cache_logf_t = jnp.swapaxes(cache_logf, 2, 3)
    fg = final_g.reshape(1, d).astype(F32)
    sel = _bias_feature_matrix()

    yp = x_prompt
    ys = x_sample.reshape(n, d)
    sp, ss = [], []
    kv_state = None
    for l in range(depth):
        lw = _layer_weights(l, norm_g, w_in, b_f, rg_conv_w, rg_conv_b, rg_wa, rg_ba, rg_wx,
                            rg_bx, rg_lambda, sc_conv_w)
        wo = w_out[l].astype(BF16)
        last = l == depth - 1

        qp, kp, k_all, v_all, vt, gsa, mls, logf, rgh, rgc, scc = _prompt_in(
            yp, lw, sel, l, depth, kv_state)
        kv_state = (k_all, v_all)
        qs, ks, vs, gsas, mlss, logfs, rghs, rgcs, sccs = _sample_in(
            ys, lw, state_rg_h[l], jnp.swapaxes(state_rg_conv[l], 0, 1),
            jnp.swapaxes(state_sc_conv[l], 0, 1))
        heads = lambda a: a.reshape(n, N_HEADS, HEAD_DIM)
        ma, mas = _attn_decode(l, page_table, qp, kp, vt, gsa, cache_k, cache_v, cache_logf_t,
                               heads(qs), heads(ks), heads(vs), logfs.reshape(n, N_HEADS, 1),
                               heads(gsas))
        yp = _out_proj(yp.reshape(b * s, d), ma.reshape(b * s, ATT_WIDTH),
                       mls.reshape(b * s, LRU_WIDTH + SC_WIDTH), wo, fg,
                       rows=OUT_ROWS, final_norm=last).reshape(b, s, d)
        sp.append((logf, rgh.reshape(b, LRU_WIDTH), rgc, scc))
        ys = _out_proj(ys, mas.reshape(n, ATT_WIDTH), mlss, wo, fg, rows=n, final_norm=last)
        ss.append((ks.reshape(n, 1, N_HEADS, HEAD_DIM), vs.reshape(n, 1, N_HEADS, HEAD_DIM),
                   logfs.reshape(n, 1, N_HEADS), rghs, jnp.swapaxes(rgcs, 0, 1),
                   jnp.swapaxes(sccs, 0, 1)))

    stack = lambda states, i: jnp.stack([st[i] for st in states])
    return (yp, ys.reshape(n, 1, d),
            kv_state[0], kv_state[1], stack(sp, 0), stack(sp, 1), stack(sp, 2), stack(sp, 3),
            stack(ss, 0), stack(ss, 1), stack(ss, 2), stack(ss, 3), stack(ss, 4), stack(ss, 5))
```

```python
import functools

import numpy as np
import jax
import jax.numpy as jnp
from jax import lax
from jax.experimental import pallas as pl
from jax.experimental.pallas import tpu as pltpu

F32 = jnp.float32
BF16 = jnp.bfloat16

N_HEADS = 8
HEAD_DIM = 128
ATT_WIDTH = N_HEADS * HEAD_DIM
LRU_WIDTH = 512
LRU_BLOCKS = 8
LRU_CONV = 4
LRU_C = 8.0
SC_WIDTH = 512
SC_CONV = 3
EPS = 1e-6
_OFF_FL = 3 * ATT_WIDTH
_OFF_GA = _OFF_FL + N_HEADS
_OFF_XR = _OFF_GA + ATT_WIDTH

LANES = 128
SUBLANES = 8
VMEM_LIMIT_BYTES = 60 * 1024 * 1024

PROJ_ROWS = 512
ATTN_BLOCK = 512
ATTN_SUB = 128
OUT_ROWS = 512
DEC_PAGES = 16
PROJ_COLS = 2 * HEAD_DIM

LOG2E = 1.4426950408889634
AUG_DIM = 2 * HEAD_DIM
_SPLIT_LANES = (0, 16, 32)
_ONES_LANE = N_HEADS


def _sigmoid(x):
    return 0.5 * jnp.tanh(0.5 * x) + 0.5


def _silu(x):
    return x * _sigmoid(x)


def _log_sigmoid(x):
    return jnp.minimum(x, 0.0) - jnp.log1p(jnp.exp(-jnp.abs(x)))


def _softplus(x):
    return jnp.maximum(x, 0.0) + jnp.log1p(jnp.exp(-jnp.abs(x)))


def _rms_scale(x):
    return lax.rsqrt(jnp.mean(x * x, axis=-1, keepdims=True) + EPS)


def _dot(a, b):
    return jnp.dot(a, b, preferred_element_type=F32)


def _dot_nt(a, b):
    return lax.dot_general(a, b, (((1,), (1,)), ((), ())), preferred_element_type=F32)


def _split3(x):
    hi = x.astype(BF16).astype(F32)
    r = x - hi
    mid = r.astype(BF16).astype(F32)
    lo = (r - mid).astype(BF16).astype(F32)
    return hi, mid, lo


def _lru_gates(xc, wa, ba, wx, bx, lam):
    xb = xc.astype(BF16)
    r = _sigmoid(_dot(xb, wa) + ba)
    i = _sigmoid(_dot(xb, wx) + bx)
    a = jnp.exp((-LRU_C) * r * _softplus(-lam))
    drive = jnp.sqrt(1.0 - a * a) * (i * xc)
    return a, drive


def _bias_feature_matrix():
    m = np.zeros((LANES, 2 * ATT_WIDTH), np.float32)
    for h in range(N_HEADS):
        for part, lane0 in enumerate(_SPLIT_LANES):
            m[lane0 + h, h * HEAD_DIM + part] = 1.0
            m[lane0 + h, ATT_WIDTH + h * HEAD_DIM + 3 + part] = -1.0
            m[_ONES_LANE, h * HEAD_DIM + 3 + part] = 1.0
            m[_ONES_LANE, ATT_WIDTH + h * HEAD_DIM + part] = 1.0
    return jnp.asarray(m, BF16)


def _prompt_in_kernel(x_ref, g_ref, wqkvg_ref, wrest_ref, wfl_ref, bf_ref, sel_ref, cw_ref, cb_ref,
                      wa_ref, ba_ref, wx_ref, bx_ref, lam_ref, scw_ref, *refs, rows, n_state_in):
    (qp_ref, kp_ref, k_ref, v_ref, vt_ref, gsa_ref, mls_ref, logf_ref, rgh_ref, rgc_ref, scc_ref,
     ext_rg, ext_sc, hcar_ref, ccar_ref) = refs[n_state_in:]
    t = pl.program_id(1)
    last_t = pl.num_programs(1) - 1

    @pl.when(t == 0)
    def _():
        ext_rg[0:SUBLANES, :] = jnp.zeros((SUBLANES, LRU_WIDTH), F32)
        ext_sc[0:SUBLANES, :] = jnp.zeros((SUBLANES, SC_WIDTH), F32)
        hcar_ref[...] = jnp.zeros_like(hcar_ref)
        ccar_ref[...] = jnp.zeros_like(ccar_ref)

    x = x_ref[...]
    h = (x * _rms_scale(x) * g_ref[...]).astype(BF16)

    logf = _log_sigmoid(_dot(h, wfl_ref[...]) + bf_ref[...])
    logf_ref[...] = logf[:, 0:N_HEADS]
    row = lax.broadcasted_iota(jnp.int32, (rows, LANES), 0)
    lane = lax.broadcasted_iota(jnp.int32, (rows, LANES), 1)
    cs = logf
    shift = 1
    while shift < rows:
        cs = cs + jnp.where(row >= shift, pltpu.roll(cs, shift, 0), 0.0)
        shift *= 2
    cs = cs + ccar_ref[...]
    ccar_ref[...] = cs[rows - 1:rows, :]
    cx = jnp.where(lane < N_HEADS, cs * LOG2E, jnp.where(lane == _ONES_LANE, 1.0, 0.0))
    hi, mid, lo = _split3(cx)
    packed = (hi + pltpu.roll(mid, _SPLIT_LANES[1], 1)
              + pltpu.roll(lo, _SPLIT_LANES[2], 1)).astype(BF16)

    ch = PROJ_COLS
    heads_per_chunk = ch // HEAD_DIM

    def attn_chunk(c):
        lo_c, hi_c = c * ch, (c + 1) * ch
        q = _dot(h, wqkvg_ref[:, lo_c:hi_c]) * (HEAD_DIM ** -0.5 * LOG2E)
        k = _dot(h, wqkvg_ref[:, ATT_WIDTH + lo_c:ATT_WIDTH + hi_c])
        v = _dot(h, wqkvg_ref[:, 2 * ATT_WIDTH + lo_c:2 * ATT_WIDTH + hi_c])
        ga = _dot(h, wqkvg_ref[:, 3 * ATT_WIDTH + lo_c:3 * ATT_WIDTH + hi_c])
        augq = _dot(packed, sel_ref[:, lo_c:hi_c])
        augk = _dot(packed, sel_ref[:, ATT_WIDTH + lo_c:ATT_WIDTH + hi_c])
        vt_ref[lo_c:hi_c, :] = v.T.astype(BF16)
        gsa_ref[:, lo_c:hi_c] = _silu(ga).astype(BF16)
        for hh in range(heads_per_chunk):
            hd = c * heads_per_chunk + hh
            src = slice(hh * HEAD_DIM, (hh + 1) * HEAD_DIM)
            qp_ref[:, hd * AUG_DIM:hd * AUG_DIM + HEAD_DIM] = q[:, src].astype(BF16)
            qp_ref[:, hd * AUG_DIM + HEAD_DIM:(hd + 1) * AUG_DIM] = augq[:, src].astype(BF16)
            kp_ref[:, hd * AUG_DIM:hd * AUG_DIM + HEAD_DIM] = k[:, src].astype(BF16)
            kp_ref[:, hd * AUG_DIM + HEAD_DIM:(hd + 1) * AUG_DIM] = augk[:, src].astype(BF16)
        return k, v

    rmod =lax.broadcasted_iota(jnp.int32, (rows, ch), 0) % SUBLANES

    def lru_chunk(c):
        cols = slice(c * ch, (c + 1) * ch)
        xr = _dot(h, wrest_ref[:, c * ch:(c + 1) * ch])
        gr = _dot(h, wrest_ref[:, LRU_WIDTH + c * ch:LRU_WIDTH + (c + 1) * ch])
        ext_rg[SUBLANES:SUBLANES + rows, cols] = xr
        xc = cb_ref[:, cols] + cw_ref[LRU_CONV - 1:LRU_CONV, cols] * xr
        for j in range(LRU_CONV - 1):
            o = SUBLANES - (LRU_CONV - 1) + j
            xc = xc + cw_ref[j:j + 1, cols] * ext_rg[o:o + rows, cols]
        tail_c = ext_rg[rows:rows + SUBLANES, cols]
        ext_rg[0:SUBLANES, cols] = tail_c
        a, drive = _lru_gates(xc, wa_ref[cols, cols], ba_ref[:, cols], wx_ref[cols, cols],
                              bx_ref[:, cols], lam_ref[:, cols])
        shift = 1
        while shift < SUBLANES:
            keep = rmod >= shift
            a_sh = pltpu.roll(a, shift, 0)
            d_sh = pltpu.roll(drive, shift, 0)
            drive = jnp.where(keep, a * d_sh + drive, drive)
            a = jnp.where(keep, a * a_sh, a)
            shift *= 2
        hprev = hcar_ref[:, cols]
        groups = []
        for gi in range(rows // SUBLANES):
            sl = slice(gi * SUBLANES, (gi + 1) * SUBLANES)
            hg = a[sl, :] * hprev + drive[sl, :]
            groups.append(hg)
            hprev = hg[SUBLANES - 1:SUBLANES, :]
        hcar_ref[:, cols] = hprev
        lru = jnp.concatenate(groups, axis=0)
        mls_ref[:, cols] = (lru * _silu(gr)).astype(BF16)
        return hprev, tail_c

    def conv_chunk(c):
        cols = slice(c * ch, (c + 1) * ch)
        o0 = 2 * LRU_WIDTH + c * ch
        bc = _dot(h, wrest_ref[:, o0:o0 + ch])
        cc = _dot(h, wrest_ref[:, o0 + SC_WIDTH:o0 + SC_WIDTH + ch])
        hc = _dot(h, wrest_ref[:, o0 + 2 * SC_WIDTH:o0 + 2 * SC_WIDTH + ch])
        gc = _dot(h, wrest_ref[:, o0 + 3 * SC_WIDTH:o0 + 3 * SC_WIDTH + ch])
        u = cc * hc
        ext_sc[SUBLANES:SUBLANES + rows, cols] = u
        cu = scw_ref[SC_CONV - 1:SC_CONV, cols] * u
        for j in range(SC_CONV - 1):
            o = SUBLANES - (SC_CONV - 1) + j
            cu = cu + scw_ref[j:j + 1, cols] * ext_sc[o:o + rows, cols]
        tail_c = ext_sc[rows:rows + SUBLANES, cols]
        ext_sc[0:SUBLANES, cols] = tail_c
        mls_ref[:, LRU_WIDTH + c * ch:LRU_WIDTH + (c + 1) * ch] = (bc * cu * _silu(gc)).astype(BF16)
        return tail_c

    kv = [attn_chunk(c) for c in range(ATT_WIDTH // ch)]
    k_ref[...] = jnp.concatenate([c[0] for c in kv], axis=1).reshape(rows, N_HEADS, HEAD_DIM)
    v_ref[...] = jnp.concatenate([c[1] for c in kv], axis=1).reshape(rows, N_HEADS, HEAD_DIM)
    lru_out = [lru_chunk(c) for c in range(LRU_WIDTH // ch)]
    hlast = jnp.concatenate([o[0] for o in lru_out], axis=1)
    tail = jnp.concatenate([o[1] for o in lru_out], axis=1)
    tail_sc = jnp.concatenate([conv_chunk(c) for c in range(SC_WIDTH // ch)], axis=1)

    @pl.when(t == last_t)
    def _():
        rgh_ref[...] = hlast
        rgc_ref[...] = tail[SUBLANES - (LRU_CONV - 1):SUBLANES, :]
        scc_ref[...] = tail_sc[SUBLANES - (SC_CONV - 1):SUBLANES, :]


def _const_spec(shape):
    nd = len(shape)
    return pl.BlockSpec(shape, lambda *_: (0,) * nd, pipeline_mode=pl.Buffered(1))


def _prompt_in(x, lw, sel, layer, depth, kv_state):
    b, s, d = x.shape
    rows = PROJ_ROWS
    assert rows == ATTN_BLOCK and s % rows == 0
    nt = s // rows
    row_spec = lambda w: pl.BlockSpec((None, rows, w), lambda bi, ti: (bi, ti, 0))
    last_spec = lambda r, w: pl.BlockSpec((None, r, w), lambda bi, ti: (bi, 0, 0))
    state_spec = pl.BlockSpec((None, None, rows, N_HEADS, HEAD_DIM),
                              lambda bi, ti: (layer, bi, ti, 0, 0))
    weights = (lw["g"], lw["w_qkvg"], lw["w_rest"], lw["w_fl"], lw["b_f"], sel, lw["rg_conv_w"],
               lw["rg_conv_b"], lw["wa"], lw["ba"], lw["wx"], lw["bx"], lw["lam"], lw["sc_conv_w"])
    state_in = () if kv_state is None else tuple(kv_state)
    n_in = 1 + len(weights)
    aliases = {n_in + i: 2 + i for i in range(len(state_in))}
    out_shape = (
        jax.ShapeDtypeStruct((b, s, N_HEADS * AUG_DIM), BF16),
        jax.ShapeDtypeStruct((b, s, N_HEADS * AUG_DIM), BF16),
        jax.ShapeDtypeStruct((depth, b, s, N_HEADS, HEAD_DIM), F32),
        jax.ShapeDtypeStruct((depth, b, s, N_HEADS, HEAD_DIM), F32),
        jax.ShapeDtypeStruct((b, nt, ATT_WIDTH, rows), BF16),
        jax.ShapeDtypeStruct((b, s, ATT_WIDTH), BF16),
        jax.ShapeDtypeStruct((b, s, LRU_WIDTH + SC_WIDTH), BF16),
        jax.ShapeDtypeStruct((b, s, N_HEADS), F32),
        jax.ShapeDtypeStruct((b, 1, LRU_WIDTH), F32),
        jax.ShapeDtypeStruct((b, LRU_CONV - 1, LRU_WIDTH), F32),
        jax.ShapeDtypeStruct((b, SC_CONV - 1, SC_WIDTH), F32),
    )
    out_specs = (
        row_spec(N_HEADS * AUG_DIM), row_spec(N_HEADS * AUG_DIM),
        state_spec, state_spec,
        pl.BlockSpec((None, None, ATT_WIDTH, rows), lambda bi, ti: (bi, ti, 0, 0)),
        row_spec(ATT_WIDTH), row_spec(LRU_WIDTH + SC_WIDTH), row_spec(N_HEADS),
        last_spec(1, LRU_WIDTH), last_spec(LRU_CONV - 1, LRU_WIDTH), last_spec(SC_CONV - 1, SC_WIDTH),
    )
    return pl.pallas_call(
        functools.partial(_prompt_in_kernel, rows=rows, n_state_in=len(state_in)),
        out_shape=out_shape,
        grid=(b, nt),
        in_specs=([row_spec(d)] + [_const_spec(w.shape) for w in weights]
                  + [pl.BlockSpec(memory_space=pl.ANY)] * len(state_in)),
        out_specs=out_specs,
        scratch_shapes=[
            pltpu.VMEM((rows + SUBLANES, LRU_WIDTH), F32),
            pltpu.VMEM((rows + SUBLANES, SC_WIDTH), F32),
            pltpu.VMEM((1, LRU_WIDTH), F32),
            pltpu.VMEM((1, LANES), F32),
        ],
        input_output_aliases=aliases,
        compiler_params=pltpu.CompilerParams(
            dimension_semantics=("arbitrary", "arbitrary"), vmem_limit_bytes=VMEM_LIMIT_BYTES),
        name="prompt_in",
    )(x, *weights, *state_in)


def _attn_kv_block(qp_ref, kp_ref, vt_ref, m_ref, l_ref, acc_ref, j, masked):
    blk = ATTN_BLOCK
    sub = ATTN_SUB
    r0 = pl.multiple_of(j * blk, blk)
    for hd in range(N_HEADS):
        kblk = kp_ref[pl.ds(r0, blk), hd * AUG_DIM:(hd + 1) * AUG_DIM]
        s_blk = _dot_nt(kblk, qp_ref[:, hd * AUG_DIM:(hd + 1) * AUG_DIM])
        m_run, l_run, acc = m_ref[hd], l_ref[hd], acc_ref[hd]
        for r in range(blk // sub):
            s = s_blk[r * sub:(r + 1) * sub, :]
            if masked:
                krow = lax.broadcasted_iota(jnp.int32, (sub, blk), 0) + r * sub
                qcol = lax.broadcasted_iota(jnp.int32, (sub, blk), 1)
                s = jnp.where(krow <= qcol, s, -jnp.inf)
            m_new = jnp.maximum(m_run, jnp.max(s, axis=0, keepdims=True))
            alpha = jnp.exp2(m_run - m_new)
            p = jnp.exp2(s - m_new)
            l_run = alpha * l_run + jnp.sum(p, axis=0, keepdims=True)
            m_run = m_new
            vt = vt_ref[j, hd * HEAD_DIM:(hd + 1) * HEAD_DIM, r * sub:(r + 1) * sub]
            acc = alpha * acc + _dot(vt, p.astype(BF16))
        m_ref[hd], l_ref[hd], acc_ref[hd] = m_run, l_run, acc


def _out_kernel(x_ref, ma_ref, mls_ref, wo_ref, fg_ref, y_ref, *, final_norm):
    y = x_ref[...] + _dot(ma_ref[...].astype(BF16), wo_ref[0:ATT_WIDTH, :])
    y = y + _dot(mls_ref[...], wo_ref[ATT_WIDTH:, :])
    if final_norm:
        y = y * _rms_scale(y) * fg_ref[...]
    y_ref[...] = y


def _out_proj(x2, ma2, mls2, wo, fg, *, rows, final_norm):
    n, d = x2.shape
    spec = lambda w: pl.BlockSpec((rows, w), lambda i: (i, 0))
    return pl.pallas_call(
        functools.partial(_out_kernel, final_norm=final_norm),
        out_shape=jax.ShapeDtypeStruct((n, d), F32),
        grid=(n // rows,),
        in_specs=[spec(d), spec(ATT_WIDTH), spec(LRU_WIDTH + SC_WIDTH),
                  _const_spec(wo.shape), _const_spec(fg.shape)],
        out_specs=spec(d),
        compiler_params=pltpu.CompilerParams(
            dimension_semantics=("arbitrary",), vmem_limit_bytes=VMEM_LIMIT_BYTES),
        name="out_proj",
    )(x2, ma2, mls2, wo, fg)


def _sample_in_kernel(x_ref, g_ref, wqkvg_ref, wrest_ref, wfl_ref, bf_ref, cw_ref, cb_ref,
                      wa_ref, ba_ref, wx_ref, bx_ref, lam_ref, scw_ref, h0_ref, rgbuf_ref, scbuf_ref,
                      q_ref, k_ref, v_ref, gsa_ref, mls_ref, logf_ref, rgh_ref, rgc_ref, scc_ref):
    x = x_ref[...]
    h = (x * _rms_scale(x) * g_ref[...]).astype(BF16)
    q_ref[...] = _dot(h, wqkvg_ref[:, 0:ATT_WIDTH]) * (HEAD_DIM ** -0.5)
    k_ref[...] = _dot(h, wqkvg_ref[:, ATT_WIDTH:2 * ATT_WIDTH])
    v_ref[...] = _dot(h, wqkvg_ref[:, 2 * ATT_WIDTH:3 * ATT_WIDTH])
    gsa_ref[...] = _silu(_dot(h, wqkvg_ref[:, 3 * ATT_WIDTH:4 * ATT_WIDTH]))
    logf = _log_sigmoid(_dot(h, wfl_ref[...]) + bf_ref[...])
    logf_ref[...] = logf[:, 0:N_HEADS]

    xr = _dot(h, wrest_ref[:, 0:LRU_WIDTH])
    gr = _dot(h, wrest_ref[:, LRU_WIDTH:2 * LRU_WIDTH])
    xc = cb_ref[...] + cw_ref[LRU_CONV - 1:LRU_CONV, :] * xr
    for j in range(LRU_CONV - 1):
        xc = xc + cw_ref[j:j + 1, :] * rgbuf_ref[j]
    for j in range(LRU_CONV - 2):
        rgc_ref[j] = rgbuf_ref[j + 1]
    rgc_ref[LRU_CONV - 2] = xr
    a, drive = _lru_gates(xc, wa_ref[...], ba_ref[...], wx_ref[...], bx_ref[...], lam_ref[...])
    hn = a * h0_ref[...] + drive
    rgh_ref[...] = hn
    mls_ref[:, 0:LRU_WIDTH] = (hn * _silu(gr)).astype(BF16)

    o0 = 2 * LRU_WIDTH
    bc = _dot(h, wrest_ref[:, o0:o0 + SC_WIDTH])
    cc = _dot(h, wrest_ref[:, o0 + SC_WIDTH:o0 + 2 * SC_WIDTH])
    hc = _dot(h, wrest_ref[:, o0 + 2 * SC_WIDTH:o0 + 3 * SC_WIDTH])
    gc = _dot(h, wrest_ref[:, o0 + 3 * SC_WIDTH:o0 + 4 * SC_WIDTH])
    u = cc * hc
    cu = scw_ref[SC_CONV - 1:SC_CONV, :] * u
    for j in range(SC_CONV - 1):
        cu = cu + scw_ref[j:j + 1, :] * scbuf_ref[j]
    for j in range(SC_CONV - 2):
        scc_ref[j] = scbuf_ref[j + 1]
    scc_ref[SC_CONV - 2] = u
    mls_ref[:, LRU_WIDTH:LRU_WIDTH + SC_WIDTH] = (bc * cu * _silu(gc)).astype(BF16)


def _sample_in(x2, lw, h0, rgbuf_t, scbuf_t):
    n, _ = x2.shape
    weights = (lw["g"], lw["w_qkvg"], lw["w_rest"], lw["w_fl"], lw["b_f"], lw["rg_conv_w"],
               lw["rg_conv_b"], lw["wa"], lw["ba"], lw["wx"], lw["bx"], lw["lam"], lw["sc_conv_w"])
    out_shape = (
        jax.ShapeDtypeStruct((n, ATT_WIDTH), F32),
        jax.ShapeDtypeStruct((n, ATT_WIDTH), F32),
        jax.ShapeDtypeStruct((n, ATT_WIDTH), F32),
        jax.ShapeDtypeStruct((n, ATT_WIDTH), F32),
        jax.ShapeDtypeStruct((n, LRU_WIDTH + SC_WIDTH), BF16),
        jax.ShapeDtypeStruct((n, N_HEADS), F32),
        jax.ShapeDtypeStruct((n, LRU_WIDTH), F32),
        jax.ShapeDtypeStruct((LRU_CONV - 1, n, LRU_WIDTH), F32),
        jax.ShapeDtypeStruct((SC_CONV - 1, n, SC_WIDTH), F32),
    )
    operands = (x2,) + weights + (h0, rgbuf_t, scbuf_t)
    full = lambda shape: pl.BlockSpec(shape, lambda i, nd=len(shape): (0,) * nd)
    return pl.pallas_call(
        _sample_in_kernel,
        out_shape=out_shape,
        grid=(1,),
        in_specs=[_const_spec(o.shape) for o in operands],
        out_specs=tuple(full(o.shape) for o in out_shape),
        compiler_params=pltpu.CompilerParams(
            dimension_semantics=("arbitrary",), vmem_limit_bytes=VMEM_LIMIT_BYTES),
        name="sample_in",
    )(*operands)


def _decode_step(j, n_steps, q_ref, kn_ref, vn_ref, fn_ref, gsa_ref, page_refs, o_ref, scratch):
    npg = DEC_PAGES
    k_refs = page_refs[0:npg]
    v_refs = page_refs[npg:2 * npg]
    f_refs = page_refs[2 * npg:3 * npg]
    qb_ref, m_ref, l_ref, acc_ref, tot_ref = scratch
    page = LANES
    width = page * N_HEADS

    @pl.when(j == 0)
    def _():
        q = q_ref[...]
        qb_ref[...] = q.astype(BF16)
        m_ref[...] = jnp.sum(q * kn_ref[...], axis=1, keepdims=True)
        l_ref[...] = jnp.ones_like(l_ref)
        acc_ref[...] = vn_ref[...]
        tot_ref[...] = jnp.broadcast_to(fn_ref[...], (N_HEADS, LANES))

    ri = lax.broadcasted_iota(jnp.int32, (page, 2 * page), 0)
    ci = lax.broadcasted_iota(jnp.int32, (page, 2 * page), 1)
    u2 = jnp.where((ri > ci) | (ci >= page), 1.0, 0.0).astype(F32)
    rs = lax.broadcasted_iota(jnp.int32, (page, width), 0)
    cs = lax.broadcasted_iota(jnp.int32, (page, width), 1)
    spread = jnp.where(cs // N_HEADS == rs, 1.0, 0.0).astype(BF16)
    own = (lax.broadcasted_iota(jnp.int32, (N_HEADS, width), 1) % N_HEADS
           == lax.broadcasted_iota(jnp.int32, (N_HEADS, width), 0))

    tot = tot_ref[...]
    logf_pages = jnp.concatenate([f_refs[i][...] for i in range(npg)], axis=0)
    suf = jnp.dot(logf_pages, u2, precision=lax.Precision.HIGHEST, preferred_element_type=F32)
    bias = []
    for i in range(npg):
        rows_i = slice(i * N_HEADS, (i + 1) * N_HEADS)
        bias.append(tot + suf[rows_i, 0:page])
        tot = tot + suf[rows_i, page:2 * page]
    tot_ref[...] = tot
    hi, mid, lo = _split3(jnp.concatenate(bias, axis=0))
    parts = jnp.concatenate([hi, mid, lo], axis=0).astype(BF16)
    spread_parts = _dot(parts, spread)
    rows = npg * N_HEADS
    bias_w = spread_parts[0:rows] + spread_parts[rows:2 * rows] + spread_parts[2 * rows:3 * rows]

    qb = qb_ref[...]
    scores = []
    for i in range(npg):
        kpage = k_refs[i][...].reshape(width, HEAD_DIM).astype(BF16)
        s = _dot_nt(qb, kpage) + bias_w[i * N_HEADS:(i + 1) * N_HEADS]
        scores.append(jnp.where(own, s, -jnp.inf))
    s_all = jnp.concatenate(scores, axis=1)
    m_old = m_ref[...]
    m_new = jnp.maximum(m_old, jnp.max(s_all, axis=1, keepdims=True))
    alpha = jnp.exp(m_old - m_new)
    p = jnp.exp(s_all - m_new)
    l_ref[...] = alpha * l_ref[...] + jnp.sum(p, axis=1, keepdims=True)
    m_ref[...] = m_new
    pb = p.astype(BF16)
    acc = alpha * acc_ref[...]
    for i in range(npg):
        vpage = v_refs[i][...].reshape(width, HEAD_DIM).astype(BF16)
        acc = acc + _dot(pb[:, i * width:(i + 1) * width], vpage)
    acc_ref[...] = acc

    @pl.when(j == n_steps - 1)
    def _():
        o_ref[...] = acc / l_ref[...] * gsa_ref[...]


def _attn_decode_kernel(pt_ref, qp_ref, kp_ref, vt_ref, gsa_ref, q_ref, kn_ref, vn_ref, fn_ref,
                        gsd_ref, ck_hbm, cv_hbm, cf_hbm, o_ref, od_ref, m_ref, l_ref, acc_ref,
                        qb_ref, md_ref, ld_ref, accd_ref, tot_ref, kbuf, vbuf, fbuf, sems,
                        *, layer, n_pages):
    npg = DEC_PAGES
    nq = pl.num_programs(1)
    steps = pl.num_programs(2)
    qi = pl.program_id(1)
    c = pl.program_id(2)
    t = (pl.program_id(0) * nq + qi) * steps + c
    total = pl.num_programs(0) * nq * steps
    slot = lax.rem(t, 2)

    def page_copies(step, slot_, page_of):
        seq_ = lax.div(step, steps)
        c_ = lax.rem(step, steps)
        copies = []
        for i in range(npg):
            pid = page_of(seq_, n_pages - 1 - (c_ * npg + i))
            copies.append(pltpu.make_async_copy(
                ck_hbm.at[layer, pid], kbuf.at[slot_, i], sems.at[0, slot_]))
            copies.append(pltpu.make_async_copy(
                cv_hbm.at[layer, pid], vbuf.at[slot_, i], sems.at[1, slot_]))
            copies.append(pltpu.make_async_copy(
                cf_hbm.at[layer, pid], fbuf.at[slot_, i], sems.at[2, slot_]))
        return copies

    def start_pages(step, slot_):
        for cp in page_copies(step, slot_, lambda s_, p_: pt_ref[s_, p_]):
            cp.start()

    @pl.when(t == 0)
    def _():
        start_pages(t, 0)
        start_pages(t + 1, 1)

    @pl.when(c == 0)
    def _():
        m_ref[...] = jnp.full(m_ref.shape, -jnp.inf, F32)
        l_ref[...] = jnp.zeros(l_ref.shape, F32)
        acc_ref[...] = jnp.zeros(acc_ref.shape, F32)

    @pl.when(c < qi)
    def _():
        _attn_kv_block(qp_ref, kp_ref, vt_ref, m_ref, l_ref, acc_ref, c, False)

    @pl.when(c == qi)
    def _():
        _attn_kv_block(qp_ref, kp_ref, vt_ref, m_ref, l_ref, acc_ref, c, True)
        for hd in range(N_HEADS):
            lo, hi = hd * HEAD_DIM, (hd + 1) * HEAD_DIM
            att = (acc_ref[hd] / l_ref[hd]).T
            o_ref[:, lo:hi] = (att * gsa_ref[:, lo:hi].astype(F32)).astype(BF16)

    for cp in page_copies(t, slot, lambda s_, p_: 0):
        cp.wait()
    page_refs = ([kbuf.at[slot, i] for i in range(npg)] + [vbuf.at[slot, i] for i in range(npg)]
                 + [fbuf.at[slot, i] for i in range(npg)])
    _decode_step(c, steps, q_ref, kn_ref, vn_ref, fn_ref, gsd_ref, page_refs, od_ref,
                 (qb_ref, md_ref, ld_ref, accd_ref, tot_ref))

    @pl.when(t + 2 < total)
    def _():
        start_pages(t + 2, slot)


def _attn_decode(layer, page_table, qp, kp, vt, gsa, cache_k, cache_v, cache_logf_t,
                 q3, kn3, vn3, fn3, gsa3):
    b, s, _ = qp.shape
    blk = ATTN_BLOCK
    nq = s // blk
    n, n_pages = page_table.shape
    page = cache_k.shape[2]
    steps = n_pages // DEC_PAGES
    assert page == LANES and n_pages % DEC_PAGES == 0 and n == b * nq and steps >= nq
    assert b * nq * steps >= 2
    qspec = pl.BlockSpec((None, blk, N_HEADS * AUG_DIM), lambda bi, qi, c, pt: (bi, qi, 0))
    ospec = pl.BlockSpec((None, blk, ATT_WIDTH), lambda bi, qi, c, pt: (bi, qi, 0))
    tok = pl.BlockSpec((None, N_HEADS, HEAD_DIM), lambda bi, qi, c, pt: (bi * nq + qi, 0, 0))

    per_row = dict(pipeline_mode=pl.Buffered(1))
    in_hbm = pl.BlockSpec(memory_space=pl.ANY)
    in_specs = [qspec,
                pl.BlockSpec((None, s, N_HEADS * AUG_DIM), lambda bi, qi, c, pt: (bi, 0, 0),
                             **per_row),
                pl.BlockSpec((None, nq, ATT_WIDTH, blk), lambda bi, qi, c, pt: (bi, 0, 0, 0),
                             **per_row),
                ospec,
                tok, tok, tok,
                pl.BlockSpec((None, N_HEADS, 1), lambda bi, qi, c, pt: (bi * nq + qi, 0, 0)), tok,
                in_hbm, in_hbm, in_hbm]
    grid_spec = pltpu.PrefetchScalarGridSpec(
        num_scalar_prefetch=1,
        grid=(b, nq, steps),
        in_specs=in_specs,
        out_specs=(ospec, tok),
        scratch_shapes=[
            pltpu.VMEM((N_HEADS, 1, blk), F32),
            pltpu.VMEM((N_HEADS, 1, blk), F32),
            pltpu.VMEM((N_HEADS, HEAD_DIM, blk), F32),
            pltpu.VMEM((N_HEADS, HEAD_DIM), BF16),
            pltpu.VMEM((N_HEADS, 1), F32),
            pltpu.VMEM((N_HEADS, 1), F32),
            pltpu.VMEM((N_HEADS, HEAD_DIM), F32),
            pltpu.VMEM((N_HEADS, LANES), F32),
            pltpu.VMEM((2, DEC_PAGES, page, N_HEADS, HEAD_DIM), F32),
            pltpu.VMEM((2, DEC_PAGES, page, N_HEADS, HEAD_DIM), F32),
            pltpu.VMEM((2, DEC_PAGES, N_HEADS, page), F32),
            pltpu.SemaphoreType.DMA((3, 2)),
        ],
    )
    return pl.pallas_call(
        functools.partial(_attn_decode_kernel, layer=layer, n_pages=n_pages),
        out_shape=(jax.ShapeDtypeStruct((b, s, ATT_WIDTH), BF16),
                   jax.ShapeDtypeStruct((n, N_HEADS, HEAD_DIM), F32)),
        grid_spec=grid_spec,
        compiler_params=pltpu.CompilerParams(
            dimension_semantics=("arbitrary", "arbitrary", "arbitrary"),
            vmem_limit_bytes=VMEM_LIMIT_BYTES),
        name="attn_decode",
    )(page_table, qp, kp, vt, gsa, q3, kn3, vn3, fn3, gsa3, cache_k, cache_v, cache_logf_t)


def _block_diag(w):
    g, i, j = w.shape
    eye = jnp.eye(g, dtype=w.dtype)
    return (eye[:, None, :, None] * w[:, :, None, :]).reshape(g * i, g * j)


def _layer_weights(l, norm_g, w_in, b_f, rg_conv_w, rg_conv_b, rg_wa, rg_ba, rg_wx, rg_bx,
                   rg_lambda, sc_conv_w):
    w = w_in[l]
    row = lambda a: a.reshape(1, -1).astype(F32)
    return dict(
        g=row(norm_g[l]),
        w_qkvg=jnp.concatenate([w[:, 0:_OFF_FL], w[:, _OFF_GA:_OFF_XR]], axis=1).astype(BF16),
        w_rest=w[:, _OFF_XR:].astype(BF16),
        w_fl=jnp.pad(w[:, _OFF_FL:_OFF_GA], ((0, 0), (0, LANES - N_HEADS))).astype(BF16),
        b_f=jnp.pad(row(b_f[l]), ((0, 0), (0, LANES - N_HEADS))),
        rg_conv_w=rg_conv_w[l].astype(F32), rg_conv_b=row(rg_conv_b[l]),
        wa=_block_diag(rg_wa[l]).astype(BF16), ba=row(rg_ba[l]),
        wx=_block_diag(rg_wx[l]).astype(BF16), bx=row(rg_bx[l]),
        lam=row(rg_lambda[l]), sc_conv_w=sc_conv_w[l].astype(F32),
    )


def kernel(x_prompt, x_sample, cache_k, cache_v, cache_logf, state_rg_h, state_rg_conv, state_sc_conv, page_table, norm_g, w_in, b_f, rg_conv_w, rg_conv_b, rg_wa, rg_ba, rg_wx, rg_bx, rg_lambda, sc_conv_w, w_out, final_g):
    depth = w_in.shape[0]
    b, s, d = x_prompt.shape
    n = x_sample.shape[0]
    cache_logf_t = jnp.swapaxes(cache_logf, 2, 3)
    fg = final_g.reshape(1, d).astype(F32)
    sel = _bias_feature_matrix()

    yp = x_prompt
    ys = x_sample.reshape(n, d)
    sp, ss = [], []
    kv_state = None
    for l in range(depth):
        lw = _layer_weights(l, norm_g, w_in, b_f, rg_conv_w, rg_conv_b, rg_wa, rg_ba, rg_wx,
                            rg_bx, rg_lambda, sc_conv_w)
        wo = w_out[l].astype(BF16)
        last = l == depth - 1

        qp, kp, k_all, v_all, vt, gsa, mls, logf, rgh, rgc, scc = _prompt_in(
            yp, lw, sel, l, depth, kv_state)
        kv_state = (k_all, v_all)
        qs, ks, vs, gsas, mlss, logfs, rghs, rgcs, sccs = _sample_in(
            ys, lw, state_rg_h[l], jnp.swapaxes(state_rg_conv[l], 0, 1),
            jnp.swapaxes(state_sc_conv[l], 0, 1))
        heads = lambda a: a.reshape(n, N_HEADS, HEAD_DIM)
        ma, mas = _attn_decode(l, page_table, qp, kp, vt, gsa, cache_k, cache_v, cache_logf_t,
                               heads(qs), heads(ks), heads(vs), logfs.reshape(n, N_HEADS, 1),
                               heads(gsas))
        yp = _out_proj(yp.reshape(b * s, d), ma.reshape(b * s, ATT_WIDTH),
                       mls.reshape(b * s, LRU_WIDTH + SC_WIDTH), wo, fg,
                       rows=OUT_ROWS, final_norm=last).reshape(b, s, d)
        sp.append((logf, rgh.reshape(b, LRU_WIDTH), rgc, scc))
        ys = _out_proj(ys, mas.reshape(n, ATT_WIDTH), mlss, wo, fg, rows=n, final_norm=last)
        ss.append((ks.reshape(n, 1, N_HEADS, HEAD_DIM), vs.reshape(n, 1, N_HEADS, HEAD_DIM),
                   logfs.reshape(n, 1, N_HEADS), rghs, jnp.swapaxes(rgcs, 0, 1),
                   jnp.swapaxes(sccs, 0, 1)))

    stack = lambda states, i: jnp.stack([st[i] for st in states])
    return (yp, ys.reshape(n, 1, d),
            kv_state[0], kv_state[1], stack(sp, 0), stack(sp, 1), stack(sp, 2), stack(sp, 3),
            stack(ss, 0), stack(ss, 1), stack(ss, 2), stack(ss, 3), stack(ss, 4), stack(ss, 5))
```

```python
import functools

import numpy as np
import jax
import jax.numpy as jnp
from jax import lax
from jax.experimental import pallas as pl
from jax.experimental.pallas import tpu as pltpu

F32 = jnp.float32
BF16 = jnp.bfloat16

N_HEADS = 8
HEAD_DIM = 128
ATT_WIDTH = N_HEADS * HEAD_DIM
LRU_WIDTH = 512
LRU_BLOCKS = 8
LRU_CONV = 4
LRU_C = 8.0
SC_WIDTH = 512
SC_CONV = 3
EPS = 1e-6
_OFF_FL = 3 * ATT_WIDTH
_OFF_GA = _OFF_FL + N_HEADS
_OFF_XR = _OFF_GA + ATT_WIDTH

LANES = 128
SUBLANES = 8
VMEM_LIMIT_BYTES = 60 * 1024 * 1024

PROJ_ROWS = 512
ATTN_BLOCK = 512
ATTN_SUB = 128
OUT_ROWS = 512
DEC_PAGES = 16
PROJ_COLS = 2 * HEAD_DIM

LOG2E = 1.4426950408889634
AUG_DIM = 2 * HEAD_DIM
_SPLIT_LANES = (0, 16, 32)
_ONES_LANE = N_HEADS


def _sigmoid(x):
    return 0.5 * jnp.tanh(0.5 * x) + 0.5


def _silu(x):
    return x * _sigmoid(x)


def _log_sigmoid(x):
    return jnp.minimum(x, 0.0) - jnp.log1p(jnp.exp(-jnp.abs(x)))


def _softplus(x):
    return jnp.maximum(x, 0.0) + jnp.log1p(jnp.exp(-jnp.abs(x)))


def _rms_scale(x):
    return lax.rsqrt(jnp.mean(x * x, axis=-1, keepdims=True) + EPS)


def _dot(a, b):
    return jnp.dot(a, b, preferred_element_type=F32)


def _dot_nt(a, b):
    return lax.dot_general(a, b, (((1,), (1,)), ((), ())), preferred_element_type=F32)


def _split3(x):
    hi = x.astype(BF16).astype(F32)
    r = x - hi
    mid = r.astype(BF16).astype(F32)
    lo = (r - mid).astype(BF16).astype(F32)
    return hi, mid, lo


def _lru_gates(xc, wa, ba, wx, bx, lam):
    xb = xc.astype(BF16)
    r = _sigmoid(_dot(xb, wa) + ba)
    i = _sigmoid(_dot(xb, wx) + bx)
    a = jnp.exp((-LRU_C) * r * _softplus(-lam))
    drive = jnp.sqrt(1.0 - a * a) * (i * xc)
    return a, drive


def _bias_feature_matrix():
    m = np.zeros((LANES, 2 * ATT_WIDTH), np.float32)
    for h in range(N_HEADS):
        for part, lane0 in enumerate(_SPLIT_LANES):
            m[lane0 + h, h * HEAD_DIM + part] = 1.0
            m[lane0 + h, ATT_WIDTH + h * HEAD_DIM + 3 + part] = -1.0
            m[_ONES_LANE, h * HEAD_DIM + 3 + part] = 1.0
            m[_ONES_LANE, ATT_WIDTH + h * HEAD_DIM + part] = 1.0
    return jnp.asarray(m, BF16)


def _prompt_in_kernel(x_ref, g_ref, wqkvg_ref, wrest_ref, wfl_ref, bf_ref, sel_ref, cw_ref, cb_ref,
                      wa_ref, ba_ref, wx_ref, bx_ref, lam_ref, scw_ref, *refs, rows, n_state_in):
    (qp_ref, kp_ref, k_ref, v_ref, vt_ref, gsa_ref, mls_ref, logf_ref, rgh_ref, rgc_ref, scc_ref,
     ext_rg, ext_sc, hcar_ref, ccar_ref) = refs[n_state_in:]
    t = pl.program_id(1)
    last_t = pl.num_programs(1) - 1

    @pl.when(t == 0)
    def _():
        ext_rg[0:SUBLANES, :] = jnp.zeros((SUBLANES, LRU_WIDTH), F32)
        ext_sc[0:SUBLANES, :] = jnp.zeros((SUBLANES, SC_WIDTH), F32)
        hcar_ref[...] = jnp.zeros_like(hcar_ref)
        ccar_ref[...] = jnp.zeros_like(ccar_ref)

    x = x_ref[...]
    h = (x * _rms_scale(x) * g_ref[...]).astype(BF16)

    logf = _log_sigmoid(_dot(h, wfl_ref[...]) + bf_ref[...])
    logf_ref[...] = logf[:, 0:N_HEADS]
    row = lax.broadcasted_iota(jnp.int32, (rows, LANES), 0)
    lane = lax.broadcasted_iota(jnp.int32, (rows, LANES), 1)
    cs = logf
    shift = 1
    while shift < rows:
        cs = cs + jnp.where(row >= shift, pltpu.roll(cs, shift, 0), 0.0)
        shift *= 2
    cs = cs + ccar_ref[...]
    ccar_ref[...] = cs[rows - 1:rows, :]
    cx = jnp.where(lane < N_HEADS, cs * LOG2E, jnp.where(lane == _ONES_LANE, 1.0, 0.0))
    hi, mid, lo = _split3(cx)
    packed = (hi + pltpu.roll(mid, _SPLIT_LANES[1], 1)
              + pltpu.roll(lo, _SPLIT_LANES[2], 1)).astype(BF16)

    ch = PROJ_COLS
    heads_per_chunk = ch // HEAD_DIM

    def attn_chunk(c):
        lo_c, hi_c = c * ch, (c + 1) * ch
        q = _dot(h, wqkvg_ref[:, lo_c:hi_c]) * (HEAD_DIM ** -0.5 * LOG2E)
        k = _dot(h, wqkvg_ref[:, ATT_WIDTH + lo_c:ATT_WIDTH + hi_c])
        v = _dot(h, wqkvg_ref[:, 2 * ATT_WIDTH + lo_c:2 * ATT_WIDTH + hi_c])
        ga = _dot(h, wqkvg_ref[:, 3 * ATT_WIDTH + lo_c:3 * ATT_WIDTH + hi_c])
        augq = _dot(packed, sel_ref[:, lo_c:hi_c])
        augk = _dot(packed, sel_ref[:, ATT_WIDTH + lo_c:ATT_WIDTH + hi_c])
        vt_ref[lo_c:hi_c, :] = v.T.astype(BF16)
        gsa_ref[:, lo_c:hi_c] = _silu(ga).astype(BF16)
        for hh in range(heads_per_chunk):
            hd = c * heads_per_chunk + hh
            src = slice(hh * HEAD_DIM, (hh + 1) * HEAD_DIM)
            qp_ref[:, hd * AUG_DIM:hd * AUG_DIM + HEAD_DIM] = q[:, src].astype(BF16)
            qp_ref[:, hd * AUG_DIM + HEAD_DIM:(hd + 1) * AUG_DIM] = augq[:, src].astype(BF16)
            kp_ref[:, hd * AUG_DIM:hd * AUG_DIM + HEAD_DIM] = k[:, src].astype(BF16)
            kp_ref[:, hd * AUG_DIM + HEAD_DIM:(hd + 1) * AUG_DIM] = augk[:, src].astype(BF16)
        return k, v

    rmod =lax.broadcasted_iota(jnp.int32, (rows, ch), 0) % SUBLANES

    def lru_chunk(c):
        cols = slice(c * ch, (c + 1) * ch)
        xr = _dot(h, wrest_ref[:, c * ch:(c + 1) * ch])
        gr = _dot(h, wrest_ref[:, LRU_WIDTH + c * ch:LRU_WIDTH + (c + 1) * ch])
        ext_rg[SUBLANES:SUBLANES + rows, cols] = xr
        xc = cb_ref[:, cols] + cw_ref[LRU_CONV - 1:LRU_CONV, cols] * xr
        for j in range(LRU_CONV - 1):
            o = SUBLANES - (LRU_CONV - 1) + j
            xc = xc + cw_ref[j:j + 1, cols] * ext_rg[o:o + rows, cols]
        tail_c = ext_rg[rows:rows + SUBLANES, cols]
        ext_rg[0:SUBLANES, cols] = tail_c
        a, drive = _lru_gates(xc, wa_ref[cols, cols], ba_ref[:, cols], wx_ref[cols, cols],
                              bx_ref[:, cols], lam_ref[:, cols])
        shift = 1
        while shift < SUBLANES:
            keep = rmod >= shift
            a_sh = pltpu.roll(a, shift, 0)
            d_sh = pltpu.roll(drive, shift, 0)
            drive = jnp.where(keep, a * d_sh + drive, drive)
            a = jnp.where(keep, a * a_sh, a)
            shift *= 2
        hprev = hcar_ref[:, cols]
        groups = []
        for gi in range(rows // SUBLANES):
            sl = slice(gi * SUBLANES, (gi + 1) * SUBLANES)
            hg = a[sl, :] * hprev + drive[sl, :]
            groups.append(hg)
            hprev = hg[SUBLANES - 1:SUBLANES, :]
        hcar_ref[:, cols] = hprev
        lru = jnp.concatenate(groups, axis=0)
        mls_ref[:, cols] = (lru * _silu(gr)).astype(BF16)
        return hprev, tail_c

    def conv_chunk(c):
        cols = slice(c * ch, (c + 1) * ch)
        o0 = 2 * LRU_WIDTH + c * ch
        bc = _dot(h, wrest_ref[:, o0:o0 + ch])
        cc = _dot(h, wrest_ref[:, o0 + SC_WIDTH:o0 + SC_WIDTH + ch])
        hc = _dot(h, wrest_ref[:, o0 + 2 * SC_WIDTH:o0 + 2 * SC_WIDTH + ch])
        gc = _dot(h, wrest_ref[:, o0 + 3 * SC_WIDTH:o0 + 3 * SC_WIDTH + ch])
        u = cc * hc
        ext_sc[SUBLANES:SUBLANES + rows, cols] = u
        cu = scw_ref[SC_CONV - 1:SC_CONV, cols] * u
        for j in range(SC_CONV - 1):
            o = SUBLANES - (SC_CONV - 1) + j
            cu = cu + scw_ref[j:j + 1, cols] * ext_sc[o:o + rows, cols]
        tail_c = ext_sc[rows:rows + SUBLANES, cols]
        ext_sc[0:SUBLANES, cols] = tail_c
        mls_ref[:, LRU_WIDTH + c * ch:LRU_WIDTH + (c + 1) * ch] = (bc * cu * _silu(gc)).astype(BF16)
        return tail_c

    kv = [attn_chunk(c) for c in range(ATT_WIDTH // ch)]
    k_ref[...] = jnp.concatenate([c[0] for c in kv], axis=1).reshape(rows, N_HEADS, HEAD_DIM)
    v_ref[...] = jnp.concatenate([c[1] for c in kv], axis=1).reshape(rows, N_HEADS, HEAD_DIM)
    lru_out = [lru_chunk(c) for c in range(LRU_WIDTH // ch)]
    hlast = jnp.concatenate([o[0] for o in lru_out], axis=1)
    tail = jnp.concatenate([o[1] for o in lru_out], axis=1)
    tail_sc = jnp.concatenate([conv_chunk(c) for c in range(SC_WIDTH // ch)], axis=1)

    @pl.when(t == last_t)
    def _():
        rgh_ref[...] = hlast
        rgc_ref[...] = tail[SUBLANES - (LRU_CONV - 1):SUBLANES, :]
        scc_ref[...] = tail_sc[SUBLANES - (SC_CONV - 1):SUBLANES, :]


def _const_spec(shape):
    nd = len(shape)
    return pl.BlockSpec(shape, lambda *_: (0,) * nd, pipeline_mode=pl.Buffered(1))


def _prompt_in(x, lw, sel, layer, depth, kv_state):
    b, s, d = x.shape
    rows = PROJ_ROWS
    assert rows == ATTN_BLOCK and s % rows == 0
    nt = s // rows
    row_spec = lambda w: pl.BlockSpec((None, rows, w), lambda bi, ti: (bi, ti, 0))
    last_spec = lambda r, w: pl.BlockSpec((None, r, w), lambda bi, ti: (bi, 0, 0))
    state_spec = pl.BlockSpec((None, None, rows, N_HEADS, HEAD_DIM),
                              lambda bi, ti: (layer, bi, ti, 0, 0))
    weights = (lw["g"], lw["w_qkvg"], lw["w_rest"], lw["w_fl"], lw["b_f"], sel, lw["rg_conv_w"],
               lw["rg_conv_b"], lw["wa"], lw["ba"], lw["wx"], lw["bx"], lw["lam"], lw["sc_conv_w"])
    state_in = () if kv_state is None else tuple(kv_state)
    n_in = 1 + len(weights)
    aliases = {n_in + i: 2 + i for i in range(len(state_in))}
    out_shape = (
        jax.ShapeDtypeStruct((b, s, N_HEADS * AUG_DIM), BF16),
        jax.ShapeDtypeStruct((b, s, N_HEADS * AUG_DIM), BF16),
        jax.ShapeDtypeStruct((depth, b, s, N_HEADS, HEAD_DIM), F32),
        jax.ShapeDtypeStruct((depth, b, s, N_HEADS, HEAD_DIM), F32),
        jax.ShapeDtypeStruct((b, nt, ATT_WIDTH, rows), BF16),
        jax.ShapeDtypeStruct((b, s, ATT_WIDTH), BF16),
        jax.ShapeDtypeStruct((b, s, LRU_WIDTH + SC_WIDTH), BF16),
        jax.ShapeDtypeStruct((b, s, N_HEADS), F32),
        jax.ShapeDtypeStruct((b, 1, LRU_WIDTH), F32),
        jax.ShapeDtypeStruct((b, LRU_CONV - 1, LRU_WIDTH), F32),
        jax.ShapeDtypeStruct((b, SC_CONV - 1, SC_WIDTH), F32),
    )
    out_specs = (
        row_spec(N_HEADS * AUG_DIM), row_spec(N_HEADS * AUG_DIM),
        state_spec, state_spec,
        pl.BlockSpec((None, None, ATT_WIDTH, rows), lambda bi, ti: (bi, ti, 0, 0)),
        row_spec(ATT_WIDTH), row_spec(LRU_WIDTH + SC_WIDTH), row_spec(N_HEADS),
        last_spec(1, LRU_WIDTH), last_spec(LRU_CONV - 1, LRU_WIDTH), last_spec(SC_CONV - 1, SC_WIDTH),
    )
    return pl.pallas_call(
        functools.partial(_prompt_in_kernel, rows=rows, n_state_in=len(state_in)),
        out_shape=out_shape,
        grid=(b, nt),
        in_specs=([row_spec(d)] + [_const_spec(w.shape) for w in weights]
                  + [pl.BlockSpec(memory_space=pl.ANY)] * len(state_in)),
        out_specs=out_specs,
        scratch_shapes=[
            pltpu.VMEM((rows + SUBLANES, LRU_WIDTH), F32),
            pltpu.VMEM((rows + SUBLANES, SC_WIDTH), F32),
            pltpu.VMEM((1, LRU_WIDTH), F32),
            pltpu.VMEM((1, LANES), F32),
        ],
        input_output_aliases=aliases,
        compiler_params=pltpu.CompilerParams(
            dimension_semantics=("arbitrary", "arbitrary"), vmem_limit_bytes=VMEM_LIMIT_BYTES),
        name="prompt_in",
    )(x, *weights, *state_in)


def _attn_kv_block(qp_ref, kp_ref, vt_ref, m_ref, l_ref, acc_ref, j, masked):
    blk = ATTN_BLOCK
    sub = ATTN_SUB
    r0 = pl.multiple_of(j * blk, blk)
    for hd in range(N_HEADS):
        kblk = kp_ref[pl.ds(r0, blk), hd * AUG_DIM:(hd + 1) * AUG_DIM]
        s_blk = _dot_nt(kblk, qp_ref[:, hd * AUG_DIM:(hd + 1) * AUG_DIM])
        m_run, l_run, acc = m_ref[hd], l_ref[hd], acc_ref[hd]
        for r in range(blk // sub):
            s = s_blk[r * sub:(r + 1) * sub, :]
            if masked:
                krow = lax.broadcasted_iota(jnp.int32, (sub, blk), 0) + r * sub
                qcol = lax.broadcasted_iota(jnp.int32, (sub, blk), 1)
                s = jnp.where(krow <= qcol, s, -jnp.inf)
            m_new = jnp.maximum(m_run, jnp.max(s, axis=0, keepdims=True))
            alpha = jnp.exp2(m_run - m_new)
            p = jnp.exp2(s - m_new)
            l_run = alpha * l_run + jnp.sum(p, axis=0, keepdims=True)
            m_run = m_new
            vt = vt_ref[j, hd * HEAD_DIM:(hd + 1) * HEAD_DIM, r * sub:(r + 1) * sub]
            acc = alpha * acc + _dot(vt, p.astype(BF16))
        m_ref[hd], l_ref[hd], acc_ref[hd] = m_run, l_run, acc


def _out_kernel(x_ref, ma_ref, mls_ref, wo_ref, fg_ref, y_ref, *, final_norm):
    y = x_ref[...] + _dot(ma_ref[...].astype(BF16), wo_ref[0:ATT_WIDTH, :])
    y = y + _dot(mls_ref[...], wo_ref[ATT_WIDTH:, :])
    if final_norm:
        y = y * _rms_scale(y) * fg_ref[...]
    y_ref[...] = y


def _out_proj(x2, ma2, mls2, wo, fg, *, rows, final_norm):
    n, d = x2.shape
    spec = lambda w: pl.BlockSpec((rows, w), lambda i: (i, 0))
    return pl.pallas_call(
        functools.partial(_out_kernel, final_norm=final_norm),
        out_shape=jax.ShapeDtypeStruct((n, d), F32),
        grid=(n // rows,),
        in_specs=[spec(d), spec(ATT_WIDTH), spec(LRU_WIDTH + SC_WIDTH),
                  _const_spec(wo.shape), _const_spec(fg.shape)],
        out_specs=spec(d),
        compiler_params=pltpu.CompilerParams(
            dimension_semantics=("arbitrary",), vmem_limit_bytes=VMEM_LIMIT_BYTES),
        name="out_proj",
    )(x2, ma2, mls2, wo, fg)


def _sample_in_kernel(x_ref, g_ref, wqkvg_ref, wrest_ref, wfl_ref, bf_ref, cw_ref, cb_ref,
                      wa_ref, ba_ref, wx_ref, bx_ref, lam_ref, scw_ref, h0_ref, rgbuf_ref, scbuf_ref,
                      q_ref, k_ref, v_ref, gsa_ref, mls_ref, logf_ref, rgh_ref, rgc_ref, scc_ref):
    x = x_ref[...]
    h = (x * _rms_scale(x) * g_ref[...]).astype(BF16)
    q_ref[...] = _dot(h, wqkvg_ref[:, 0:ATT_WIDTH]) * (HEAD_DIM ** -0.5)
    k_ref[...] = _dot(h, wqkvg_ref[:, ATT_WIDTH:2 * ATT_WIDTH])
    v_ref[...] = _dot(h, wqkvg_ref[:, 2 * ATT_WIDTH:3 * ATT_WIDTH])
    gsa_ref[...] = _silu(_dot(h, wqkvg_ref[:, 3 * ATT_WIDTH:4 * ATT_WIDTH]))
    logf = _log_sigmoid(_dot(h, wfl_ref[...]) + bf_ref[...])
    logf_ref[...] = logf[:, 0:N_HEADS]

    xr = _dot(h, wrest_ref[:, 0:LRU_WIDTH])
    gr = _dot(h, wrest_ref[:, LRU_WIDTH:2 * LRU_WIDTH])
    xc = cb_ref[...] + cw_ref[LRU_CONV - 1:LRU_CONV, :] * xr
    for j in range(LRU_CONV - 1):
        xc = xc + cw_ref[j:j + 1, :] * rgbuf_ref[j]
    for j in range(LRU_CONV - 2):
        rgc_ref[j] = rgbuf_ref[j + 1]
    rgc_ref[LRU_CONV - 2] = xr
    a, drive = _lru_gates(xc, wa_ref[...], ba_ref[...], wx_ref[...], bx_ref[...], lam_ref[...])
    hn = a * h0_ref[...] + drive
    rgh_ref[...] = hn
    mls_ref[:, 0:LRU_WIDTH] = (hn * _silu(gr)).astype(BF16)

    o0 = 2 * LRU_WIDTH
    bc = _dot(h, wrest_ref[:, o0:o0 + SC_WIDTH])
    cc = _dot(h, wrest_ref[:, o0 + SC_WIDTH:o0 + 2 * SC_WIDTH])
    hc = _dot(h, wrest_ref[:, o0 + 2 * SC_WIDTH:o0 + 3 * SC_WIDTH])
    gc = _dot(h, wrest_ref[:, o0 + 3 * SC_WIDTH:o0 + 4 * SC_WIDTH])
    u = cc * hc
    cu = scw_ref[SC_CONV - 1:SC_CONV, :] * u
    for j in range(SC_CONV - 1):
        cu = cu + scw_ref[j:j + 1, :] * scbuf_ref[j]
    for j in range(SC_CONV - 2):
        scc_ref[j] = scbuf_ref[j + 1]
    scc_ref[SC_CONV - 2] = u
    mls_ref[:, LRU_WIDTH:LRU_WIDTH + SC_WIDTH] = (bc * cu * _silu(gc)).astype(BF16)


def _sample_in(x2, lw, h0, rgbuf_t, scbuf_t):
    n, _ = x2.shape
    weights = (lw["g"], lw["w_qkvg"], lw["w_rest"], lw["w_fl"], lw["b_f"], lw["rg_conv_w"],
               lw["rg_conv_b"], lw["wa"], lw["ba"], lw["wx"], lw["bx"], lw["lam"], lw["sc_conv_w"])
    out_shape = (
        jax.ShapeDtypeStruct((n, ATT_WIDTH), F32),
        jax.ShapeDtypeStruct((n, ATT_WIDTH), F32),
        jax.ShapeDtypeStruct((n, ATT_WIDTH), F32),
        jax.ShapeDtypeStruct((n, ATT_WIDTH), F32),
        jax.ShapeDtypeStruct((n, LRU_WIDTH + SC_WIDTH), BF16),
        jax.ShapeDtypeStruct((n, N_HEADS), F32),
        jax.ShapeDtypeStruct((n, LRU_WIDTH), F32),
        jax.ShapeDtypeStruct((LRU_CONV - 1, n, LRU_WIDTH), F32),
        jax.ShapeDtypeStruct((SC_CONV - 1, n, SC_WIDTH), F32),
    )
    operands = (x2,) + weights + (h0, rgbuf_t, scbuf_t)
    full = lambda shape: pl.BlockSpec(shape, lambda i, nd=len(shape): (0,) * nd)
    return pl.pallas_call(
        _sample_in_kernel,
        out_shape=out_shape,
        grid=(1,),
        in_specs=[_const_spec(o.shape) for o in operands],
        out_specs=tuple(full(o.shape) for o in out_shape),
        compiler_params=pltpu.CompilerParams(
            dimension_semantics=("arbitrary",), vmem_limit_bytes=VMEM_LIMIT_BYTES),
        name="sample_in",
    )(*operands)


def _decode_step(j, n_steps, q_ref, kn_ref, vn_ref, fn_ref, gsa_ref, page_refs, o_ref, scratch):
    npg = DEC_PAGES
    k_refs = page_refs[0:npg]
    v_refs = page_refs[npg:2 * npg]
    f_refs = page_refs[2 * npg:3 * npg]
    qb_ref, m_ref, l_ref, acc_ref, tot_ref = scratch
    page = LANES
    width = page * N_HEADS

    @pl.when(j == 0)
    def _():
        q = q_ref[...]
        qb_ref[...] = q.astype(BF16)
        m_ref[...] = jnp.sum(q * kn_ref[...], axis=1, keepdims=True)
        l_ref[...] = jnp.ones_like(l_ref)
        acc_ref[...] = vn_ref[...]
        tot_ref[...] = jnp.broadcast_to(fn_ref[...], (N_HEADS, LANES))

    ri = lax.broadcasted_iota(jnp.int32, (page, 2 * page), 0)
    ci = lax.broadcasted_iota(jnp.int32, (page, 2 * page), 1)
    u2 = jnp.where((ri > ci) | (ci >= page), 1.0, 0.0).astype(F32)
    rs = lax.broadcasted_iota(jnp.int32, (page, width), 0)
    cs = lax.broadcasted_iota(jnp.int32, (page, width), 1)
    spread = jnp.where(cs // N_HEADS == rs, 1.0, 0.0).astype(BF16)
    own = (lax.broadcasted_iota(jnp.int32, (N_HEADS, width), 1) % N_HEADS
           == lax.broadcasted_iota(jnp.int32, (N_HEADS, width), 0))

    tot = tot_ref[...]
    logf_pages = jnp.concatenate([f_refs[i][...] for i in range(npg)], axis=0)
    suf = jnp.dot(logf_pages, u2, precision=lax.Precision.HIGHEST, preferred_element_type=F32)
    bias = []
    for i in range(npg):
        rows_i = slice(i * N_HEADS, (i + 1) * N_HEADS)
        bias.append(tot + suf[rows_i, 0:page])
        tot = tot + suf[rows_i, page:2 * page]
    tot_ref[...] = tot
    hi, mid, lo = _split3(jnp.concatenate(bias, axis=0))
    parts = jnp.concatenate([hi, mid, lo], axis=0).astype(BF16)
    spread_parts = _dot(parts, spread)
    rows = npg * N_HEADS
    bias_w = spread_parts[0:rows] + spread_parts[rows:2 * rows] + spread_parts[2 * rows:3 * rows]

    qb = qb_ref[...]
    scores = []
    for i in range(npg):
        kpage = k_refs[i][...].reshape(width, HEAD_DIM).astype(BF16)
        s = _dot_nt(qb, kpage) + bias_w[i * N_HEADS:(i + 1) * N_HEADS]
        scores.append(jnp.where(own, s, -jnp.inf))
    s_all = jnp.concatenate(scores, axis=1)
    m_old = m_ref[...]
    m_new = jnp.maximum(m_old, jnp.max(s_all, axis=1, keepdims=True))
    alpha = jnp.exp(m_old - m_new)
    p = jnp.exp(s_all - m_new)
    l_ref[...] = alpha * l_ref[...] + jnp.sum(p, axis=1, keepdims=True)
    m_ref[...] = m_new
    pb = p.astype(BF16)
    acc = alpha * acc_ref[...]
    for i in range(npg):
        vpage = v_refs[i][...].reshape(width, HEAD_DIM).astype(BF16)
        acc = acc + _dot(pb[:, i * width:(i + 1) * width], vpage)
    acc_ref[...] = acc

    @pl.when(j == n_steps - 1)
    def _():
        o_ref[...] = acc / l_ref[...] * gsa_ref[...]


def _attn_decode_kernel(pt_ref, qp_ref, kp_ref, vt_ref, gsa_ref, q_ref, kn_ref, vn_ref, fn_ref,
                        gsd_ref, ck_hbm, cv_hbm, cf_hbm, o_ref, od_ref, m_ref, l_ref, acc_ref,
                        qb_ref, md_ref, ld_ref, accd_ref, tot_ref, kbuf, vbuf, fbuf, sems,
                        *, layer, n_pages, block_steps):
    npg = DEC_PAGES
    nq = pl.num_programs(1)
    steps = pl.num_programs(2)
    qi = pl.program_id(1)
    c = pl.program_id(2)
    t = (pl.program_id(0) * nq + qi) * steps + c
    total = pl.num_programs(0) * nq * steps
    slot = lax.rem(t, 2)

    def page_copies(step, slot_, page_of):
        seq_ = lax.div(step, steps)
        c_ = lax.rem(step, steps)
        copies = []
        for i in range(npg):
            pid = page_of(seq_, n_pages - 1 - (c_ * npg + i))
            copies.append(pltpu.make_async_copy(
                ck_hbm.at[layer, pid], kbuf.at[slot_, i], sems.at[0, slot_]))
            copies.append(pltpu.make_async_copy(
                cv_hbm.at[layer, pid], vbuf.at[slot_, i], sems.at[1, slot_]))
            copies.append(pltpu.make_async_copy(
                cf_hbm.at[layer, pid], fbuf.at[slot_, i], sems.at[2, slot_]))
        return copies

    def start_pages(step, slot_):
        for cp in page_copies(step, slot_, lambda s_, p_: pt_ref[s_, p_]):
            cp.start()

    @pl.when(t == 0)
    def _():
        start_pages(t, 0)
        start_pages(t + 1, 1)

    @pl.when(c == 0)
    def _():
        m_ref[...] = jnp.full(m_ref.shape, -jnp.inf, F32)
        l_ref[...] = jnp.zeros(l_ref.shape, F32)
        acc_ref[...] = jnp.zeros(acc_ref.shape, F32)

    has_block = False
    j = 0
    for q, steps_q in enumerate(block_steps):
        for j_q, c_q in enumerate(steps_q):
            hit = jnp.logical_and(qi == q, c == c_q)
            has_block = jnp.logical_or(has_block, hit)
            j = j + jnp.where(hit, j_q, 0)

    @pl.when(jnp.logical_and(has_block, j < qi))
    def _():
        _attn_kv_block(qp_ref, kp_ref, vt_ref, m_ref, l_ref, acc_ref, j, False)

    @pl.when(jnp.logical_and(has_block, j == qi))
    def _():
        _attn_kv_block(qp_ref, kp_ref, vt_ref, m_ref, l_ref, acc_ref, j, True)
        for hd in range(N_HEADS):
            lo, hi = hd * HEAD_DIM, (hd + 1) * HEAD_DIM
            att = (acc_ref[hd] / l_ref[hd]).T
            o_ref[:, lo:hi] = (att * gsa_ref[:, lo:hi].astype(F32)).astype(BF16)

    for cp in page_copies(t, slot, lambda s_, p_: 0):
        cp.wait()
    page_refs = ([kbuf.at[slot, i] for i in range(npg)] + [vbuf.at[slot, i] for i in range(npg)]
                 + [fbuf.at[slot, i] for i in range(npg)])
    _decode_step(c, steps, q_ref, kn_ref, vn_ref, fn_ref, gsd_ref, page_refs, od_ref,
                 (qb_ref, md_ref, ld_ref, accd_ref, tot_ref))

    @pl.when(t + 2 < total)
    def _():
        start_pages(t + 2, slot)


def _block_schedule(nq, steps):
    import itertools
    choices = [list(itertools.combinations(range(steps), q + 1)) for q in range(nq)]
    best, best_cost = None, None
    for combo in itertools.product(*choices):
        is_long = [c in combo[q] for q in range(nq) for c in range(steps)]
        cost = sum((not is_long[i]) and (not is_long[i - 1]) for i in range(len(is_long)))
        if best_cost is None or cost < best_cost:
            best, best_cost = combo, cost
    return tuple(tuple(steps_q) for steps_q in best)


def _attn_decode(layer, page_table, qp, kp, vt, gsa, cache_k, cache_v, cache_logf_t,
                 q3, kn3, vn3, fn3, gsa3):
    b, s, _ = qp.shape
    blk = ATTN_BLOCK
    nq = s // blk
    n, n_pages = page_table.shape
    page = cache_k.shape[2]
    steps = n_pages // DEC_PAGES
    assert page == LANES and n_pages % DEC_PAGES == 0 and n == b * nq and steps >= nq
    assert b * nq * steps >= 2
    qspec = pl.BlockSpec((None, blk, N_HEADS * AUG_DIM), lambda bi, qi, c, pt: (bi, qi, 0))
    ospec = pl.BlockSpec((None, blk, ATT_WIDTH), lambda bi, qi, c, pt: (bi, qi, 0))
    tok = pl.BlockSpec((None, N_HEADS, HEAD_DIM), lambda bi, qi, c, pt: (bi * nq + qi, 0, 0))

    per_row = dict(pipeline_mode=pl.Buffered(1))
    in_hbm = pl.BlockSpec(memory_space=pl.ANY)
    in_specs = [qspec,
                pl.BlockSpec((None, s, N_HEADS * AUG_DIM), lambda bi, qi, c, pt: (bi, 0, 0),
                             **per_row),
                pl.BlockSpec((None, nq, ATT_WIDTH, blk), lambda bi, qi, c, pt: (bi, 0, 0, 0),
                             **per_row),
                ospec,
                tok, tok, tok,
                pl.BlockSpec((None, N_HEADS, 1), lambda bi, qi, c, pt: (bi * nq + qi, 0, 0)), tok,
                in_hbm, in_hbm, in_hbm]
    grid_spec = pltpu.PrefetchScalarGridSpec(
        num_scalar_prefetch=1,
        grid=(b, nq, steps),
        in_specs=in_specs,
        out_specs=(ospec, tok),
        scratch_shapes=[
            pltpu.VMEM((N_HEADS, 1, blk), F32),
            pltpu.VMEM((N_HEADS, 1, blk), F32),
            pltpu.VMEM((N_HEADS, HEAD_DIM, blk), F32),
            pltpu.VMEM((N_HEADS, HEAD_DIM), BF16),
            pltpu.VMEM((N_HEADS, 1), F32),
            pltpu.VMEM((N_HEADS, 1), F32),
            pltpu.VMEM((N_HEADS, HEAD_DIM), F32),
            pltpu.VMEM((N_HEADS, LANES), F32),
            pltpu.VMEM((2, DEC_PAGES, page, N_HEADS, HEAD_DIM), F32),
            pltpu.VMEM((2, DEC_PAGES, page, N_HEADS, HEAD_DIM), F32),
            pltpu.VMEM((2, DEC_PAGES, N_HEADS, page), F32),
            pltpu.SemaphoreType.DMA((3, 2)),
        ],
    )
    return pl.pallas_call(
        functools.partial(_attn_decode_kernel, layer=layer, n_pages=n_pages,
                          block_steps=_block_schedule(nq, steps)),
        out_shape=(jax.ShapeDtypeStruct((b, s, ATT_WIDTH), BF16),
                   jax.ShapeDtypeStruct((n, N_HEADS, HEAD_DIM), F32)),
        grid_spec=grid_spec,
        compiler_params=pltpu.CompilerParams(
            dimension_semantics=("arbitrary", "arbitrary", "arbitrary"),
            vmem_limit_bytes=VMEM_LIMIT_BYTES),
        name="attn_decode",
    )(page_table, qp, kp, vt, gsa, q3, kn3, vn3, fn3, gsa3, cache_k, cache_v, cache_logf_t)


def _block_diag(w):
    g, i, j = w.shape
    eye = jnp.eye(g, dtype=w.dtype)
    return (eye[:, None, :, None] * w[:, :, None, :]).reshape(g * i, g * j)


def _layer_weights(l, norm_g, w_in, b_f, rg_conv_w, rg_conv_b, rg_wa, rg_ba, rg_wx, rg_bx,
                   rg_lambda, sc_conv_w):
    w = w_in[l]
    row = lambda a: a.reshape(1, -1).astype(F32)
    return dict(
        g=row(norm_g[l]),
        w_qkvg=jnp.concatenate([w[:, 0:_OFF_FL], w[:, _OFF_GA:_OFF_XR]], axis=1).astype(BF16),
        w_rest=w[:, _OFF_XR:].astype(BF16),
        w_fl=jnp.pad(w[:, _OFF_FL:_OFF_GA], ((0, 0), (0, LANES - N_HEADS))).astype(BF16),
        b_f=jnp.pad(row(b_f[l]), ((0, 0), (0, LANES - N_HEADS))),
        rg_conv_w=rg_conv_w[l].astype(F32), rg_conv_b=row(rg_conv_b[l]),
        wa=_block_diag(rg_wa[l]).astype(BF16), ba=row(rg_ba[l]),
        wx=_block_diag(rg_wx[l]).astype(BF16), bx=row(rg_bx[l]),
        lam=row(rg_lambda[l]), sc_conv_w=sc_conv_w[l].astype(F32),
    )


def kernel(x_prompt, x_sample, cache_k, cache_v, cache_logf, state_rg_h, state_rg_conv, state_sc_conv, page_table, norm_g, w_in, b_f, rg_conv_w, rg_conv_b, rg_wa, rg_ba, rg_wx, rg_bx, rg_lambda, sc_conv_w, w_out, final_g):
    depth = w_in.shape[0]
    b, s, d = x_prompt.shape
    n = x_sample.shape[0]
    cache_logf_t = jnp.swapaxes(cache_logf, 2, 3)
    fg = final_g.reshape(1, d).astype(F32)
    sel = _bias_feature_matrix()

    yp = x_prompt
    ys = x_sample.reshape(n, d)
    sp, ss = [], []
    kv_state = None
    for l in range(depth):
        lw = _layer_weights(l, norm_g, w_in, b_f, rg_conv_w, rg_conv_b, rg_wa, rg_ba, rg_wx,
                            rg_bx, rg_lambda, sc_conv_w)
        wo = w_out[l].astype(BF16)
        last = l == depth - 1

        qp, kp, k_all, v_all, vt, gsa, mls, logf, rgh, rgc, scc = _prompt_in(
            yp, lw, sel, l, depth, kv_state)
        kv_state = (k_all, v_all)
        qs, ks, vs, gsas, mlss, logfs, rghs, rgcs, sccs = _sample_in(
            ys, lw, state_rg_h[l], jnp.swapaxes(state_rg_conv[l], 0, 1),
            jnp.swapaxes(state_sc_conv[l], 0, 1))
        heads = lambda a: a.reshape(n, N_HEADS, HEAD_DIM)
        ma, mas = _attn_decode(l, page_table, qp, kp, vt, gsa, cache_k, cache_v, cache_logf_t,
                               heads(qs), heads(ks), heads(vs), logfs.reshape(n, N_HEADS, 1),
                               heads(gsas))
        yp = _out_proj(yp.reshape(b * s, d), ma.reshape(b * s, ATT_WIDTH),
                       mls.reshape(b * s, LRU_WIDTH + SC_WIDTH), wo, fg,
                       rows=OUT_ROWS, final_norm=last).reshape(b, s, d)
        sp.append((logf, rgh.reshape(b, LRU_WIDTH), rgc, scc))
        ys = _out_proj(ys, mas.reshape(n, ATT_WIDTH), mlss, wo, fg, rows=n, final_norm=last)
        ss.append((ks.reshape(n, 1, N_HEADS, HEAD_DIM), vs.reshape(n, 1, N_HEADS, HEAD_DIM),
                   logfs.reshape(n, 1, N_HEADS), rghs, jnp.swapaxes(rgcs, 0, 1),
                   jnp.swapaxes(sccs, 0, 1)))

    stack = lambda states, i: jnp.stack([st[i] for st in states])
    return (yp, ys.reshape(n, 1, d),
            kv_state[0], kv_state[1], stack(sp, 0), stack(sp, 1), stack(sp, 2), stack(sp, 3),
            stack(ss, 0), stack(ss, 1), stack(ss, 2), stack(ss, 3), stack(ss, 4), stack(ss, 5))
```
